```python
import math
import jax, jax.numpy as jnp
from jax import lax
import numpy as np

D_MODEL = 1024
BATCH = 8
SEQ = 2048
DEPTH = 1

CTX_LEN = 256
GRID_W = 64
D_MIX = D_MODEL
S5_WIDTH = D_MIX // 2
S5_GROUP = 16
S5_GROUPS = S5_WIDTH // S5_GROUP
S5_STATE = 64
FOURIER_WIDTH = D_MIX - S5_WIDTH
FOURIER_GROUPS = 4
FOURIER_GROUP = FOURIER_WIDTH // FOURIER_GROUPS
PEER_HEADS = 8
PEER_KEYS = 128
PEER_EXPERTS = PEER_KEYS * PEER_KEYS
PEER_TOPK = 16
PEER_DKEY = 256
PEER_HALF = PEER_DKEY // 2
PEER_CHUNK = 128
N_MOD = 6
RMS_EPS = 1e-6
DT_MIN = 1e-3
DT_MAX = 1e-1
POS_BASE = 10000.0

kernel_name = 'hymba_s5_fnet_peer_dit_layer'


def rmsnorm(x, g):
    xf = x.astype(jnp.float32)
    inv = lax.rsqrt(jnp.mean(xf * xf, axis=-1, keepdims=True) + RMS_EPS)
    return (xf * inv).astype(x.dtype) * g


def modulate(h, shift, scale):
    return h * (1 + scale) + shift


def grid_pos_embed(rows, dim, dtype):
    quarter = dim // 4
    omega = 1.0 / (POS_BASE ** (jnp.arange(quarter, dtype=jnp.float32) / quarter))
    r = jnp.repeat(jnp.arange(rows, dtype=jnp.float32), GRID_W)[:, None] * omega
    col = jnp.tile(jnp.arange(GRID_W, dtype=jnp.float32), rows)[:, None] * omega
    emb = jnp.concatenate([jnp.sin(r), jnp.cos(r), jnp.sin(col), jnp.cos(col)], axis=-1)
    return emb.astype(dtype)


def _ssm_combine(e1, e2):
    a1r, a1i, b1r, b1i = e1
    a2r, a2i, b2r, b2i = e2
    return (a2r * a1r - a2i * a1i,
            a2r * a1i + a2i * a1r,
            a2r * b1r - a2i * b1i + b2r,
            a2r * b1i + a2i * b1r + b2i)


def s5_discretise(lam_re, lam_im, log_dt, b_re, b_im):
    dt = jnp.exp(log_dt.astype(jnp.float32))[:, None]
    lr = lam_re.astype(jnp.float32)
    li = lam_im.astype(jnp.float32)
    mag = jnp.exp(lr * dt)
    ar = mag * jnp.cos(li * dt)
    ai = mag * jnp.sin(li * dt)
    den = lr * lr + li * li
    cr = ((ar - 1.0) * lr + ai * li) / den
    ci = (ai * lr - (ar - 1.0) * li) / den
    bbr = cr[..., None] * b_re - ci[..., None] * b_im
    bbi = cr[..., None] * b_im + ci[..., None] * b_re
    return ar, ai, bbr, bbi


def s5_scan(u_t, disc, h0, reverse):
    ar, ai, bbr, bbi = disc
    bu_re = jnp.einsum('lbgp,gnp->lbgn', u_t, bbr)
    bu_im = jnp.einsum('lbgp,gnp->lbgn', u_t, bbi)
    h0_re, h0_im = h0
    first = u_t.shape[0] - 1 if reverse else 0
    bu_re = bu_re.at[first].add(ar * h0_re - ai * h0_im)
    bu_im = bu_im.at[first].add(ar * h0_im + ai * h0_re)
    a_shape = (u_t.shape[0], 1) + ar.shape
    a_re = jnp.broadcast_to(ar, a_shape)
    a_im = jnp.broadcast_to(ai, a_shape)
    _, _, s_re, s_im = lax.associative_scan(_ssm_combine, (a_re, a_im, bu_re, bu_im),
                                            reverse=reverse, axis=0)
    return s_re, s_im


def s5_states(u, h0s, lam_re, lam_im, log_dt, b_re, b_im):
    bsz, length, _ = u.shape
    u_t = jnp.moveaxis(u.astype(jnp.float32).reshape(bsz, length, S5_GROUPS, S5_GROUP), 1, 0)
    fwd = s5_scan(u_t, s5_discretise(lam_re[0], lam_im[0], log_dt[0], b_re[0], b_im[0]), h0s[0], False)
    bwd = s5_scan(u_t, s5_discretise(lam_re[1], lam_im[1], log_dt[1], b_re[1], b_im[1]), h0s[1], True)
    return fwd, bwd


def s5_final_states(states):
    fwd, bwd = states
    return (fwd[0][-1], fwd[1][-1]), (bwd[0][0], bwd[1][0])


def s5_readout(s_re, s_im, c_re, c_im):
    return jnp.einsum('lbgn,gpn->lbgp', s_re, c_re) - jnp.einsum('lbgn,gpn->lbgp', s_im, c_im)


def s5_output(u, states, c_re, c_im, d_skip, w_glu, b_glu):
    bsz, length, _ = u.shape
    (fr, fi), (br, bi) = states
    y_t = s5_readout(fr, fi, c_re[0], c_im[0]) + s5_readout(br, bi, c_re[1], c_im[1])
    y = jnp.moveaxis(y_t, 0, 1).reshape(bsz, length, S5_WIDTH) + d_skip * u.astype(jnp.float32)
    y = jax.nn.gelu(y)
    y = y * jax.nn.sigmoid(y @ w_glu + b_glu)
    return y.astype(u.dtype)


def fourier_branch(f, w_fourier):
    bsz, length, _ = f.shape
    fg = f.astype(jnp.float32).reshape(bsz, length, FOURIER_GROUPS, FOURIER_GROUP)
    mixed = jnp.fft.fft2(fg, axes=(1, 3), norm='ortho').real
    out = jnp.einsum('blgc,gcd->blgd', mixed, w_fourier)
    return out.reshape(bsz, length, FOURIER_WIDTH).astype(f.dtype)


def peer(h, w_q, sub_keys, u_tab, v_tab):
    bsz, length, dim = h.shape
    blocks = h.reshape(-1, PEER_CHUNK, dim)

    def block_fn(hc):
        q = (hc @ w_q).reshape(PEER_CHUNK, PEER_HEADS, 2, PEER_HALF)
        s = jnp.einsum('chsd,hskd->chsk', q, sub_keys).astype(jnp.float32)
        sv, si = lax.top_k(s, PEER_TOPK)
        cand = sv[:, :, 0, :, None] + sv[:, :, 1, None, :]
        best, flat = lax.top_k(cand.reshape(PEER_CHUNK, PEER_HEADS, PEER_TOPK * PEER_TOPK), PEER_TOPK)
        idx1 = jnp.take_along_axis(si[:, :, 0], flat // PEER_TOPK, axis=-1)
        idx2 = jnp.take_along_axis(si[:, :, 1], flat % PEER_TOPK, axis=-1)
        expert = idx1 * PEER_KEYS + idx2
        g = jax.nn.softmax(best, axis=-1)
        act = jax.nn.gelu(jnp.einsum('cd,chkd->chk', hc, u_tab[expert]))
        return jnp.einsum('chk,chkd->cd', (g * act).astype(hc.dtype), v_tab[expert])

    return lax.map(block_fn, blocks).reshape(bsz, length, dim)


def trunk_layer(x, ctx, c_act, cc_act, w_mod, b_mod, norm1_g, norm2_g, w_in, w_out,
                lam_re, lam_im, log_dt, b_re, b_im, c_re, c_im, d_skip, w_glu, b_glu,
                w_fourier, w_q, sub_keys, u_tab, v_tab, update_ctx):
    sh1, sc1, g1, sh2, sc2, g2 = jnp.split((c_act @ w_mod + b_mod)[:, None, :], N_MOD, axis=-1)
    csh1, csc1, cg1, csh2, csc2, cg2 = jnp.split(cc_act @ w_mod + b_mod, N_MOD, axis=-1)

    h = modulate(rmsnorm(x, norm1_g), sh1, sc1)
    hc = modulate(rmsnorm(ctx, norm1_g), csh1, csc1)
    z = h @ w_in
    u, f = z[..., :S5_WIDTH], z[..., S5_WIDTH:]
    zc = hc @ w_in if update_ctx else hc @ w_in[:, :S5_WIDTH]
    uc = zc[..., :S5_WIDTH]

    zero = jnp.zeros((ctx.shape[0], S5_GROUPS, S5_STATE), jnp.float32)
    ctx_states = s5_states(uc, ((zero, zero), (zero, zero)), lam_re, lam_im, log_dt, b_re, b_im)
    lat_states = s5_states(u, s5_final_states(ctx_states), lam_re, lam_im, log_dt, b_re, b_im)

    mix = jnp.concatenate([s5_output(u, lat_states, c_re, c_im, d_skip, w_glu, b_glu),
                           fourier_branch(f, w_fourier)], axis=-1) @ w_out
    x = x + g1 * mix
    x = x + g2 * peer(modulate(rmsnorm(x, norm2_g), sh2, sc2), w_q, sub_keys, u_tab, v_tab)

    if update_ctx:
        mix_c = jnp.concatenate([s5_output(uc, ctx_states, c_re, c_im, d_skip, w_glu, b_glu),
                                 fourier_branch(zc[..., S5_WIDTH:], w_fourier)], axis=-1) @ w_out
        ctx = ctx + cg1 * mix_c
        ctx = ctx + cg2 * peer(modulate(rmsnorm(ctx, norm2_g), csh2, csc2), w_q, sub_keys, u_tab, v_tab)
    return x, ctx


def setup_inputs(seed: int = 0) -> dict:
    key = jax.random.key(seed)
    ks = jax.random.split(key, 32)
    f32 = jnp.float32
    nrm = lambda k, shape, s: jax.random.normal(k, shape, f32) * s
    L, G, N, P = DEPTH, S5_GROUPS, S5_STATE, S5_GROUP
    lam_re = -0.5 + nrm(ks[16], (L, 2, G, N), 0.01)
    lam_im = math.pi * jnp.arange(N, dtype=f32) + nrm(ks[17], (L, 2, G, N), 0.01)
    log_dt = jax.random.uniform(ks[18], (L, 2, G), f32, math.log(DT_MIN), math.log(DT_MAX))
    return {
        'x': nrm(ks[0], (BATCH, SEQ, D_MODEL), 1.0),
        'c': nrm(ks[1], (BATCH, D_MODEL), 1.0),
        'ctx': nrm(ks[2], (BATCH, CTX_LEN, D_MODEL), 1.0),
        'c_ctx': nrm(ks[3], (D_MODEL,), 1.0),
        'w_mod': nrm(ks[4], (L, D_MODEL, N_MOD * D_MODEL), D_MODEL ** -0.5),
        'b_mod': nrm(ks[5], (L, N_MOD * D_MODEL), 0.01),
        'norm1_g': 1.0 + nrm(ks[6], (L, D_MODEL), 0.05),
        'norm2_g': 1.0 + nrm(ks[7], (L, D_MODEL), 0.05),
        'w_in': nrm(ks[8], (L, D_MODEL, D_MIX), D_MODEL ** -0.5),
        'w_out': nrm(ks[9], (L, D_MIX, D_MODEL), D_MIX ** -0.5),
        's5_lam_re': lam_re,
        's5_lam_im': lam_im,
        's5_log_dt': log_dt,
        's5_b_re': nrm(ks[10], (L, 2, G, N, P), (2 * P) ** -0.5),
        's5_b_im': nrm(ks[11], (L, 2, G, N, P), (2 * P) ** -0.5),
        's5_c_re': nrm(ks[12], (L, 2, G, P, N), N ** -0.5),
        's5_c_im': nrm(ks[13], (L, 2, G, P, N), N ** -0.5),
        's5_d': nrm(ks[14], (L, S5_WIDTH), 1.0),
        'w_glu': nrm(ks[15], (L, S5_WIDTH, S5_WIDTH), S5_WIDTH ** -0.5),
        'b_glu': nrm(ks[19], (L, S5_WIDTH), 0.01),
        'w_fourier': nrm(ks[20], (L, FOURIER_GROUPS, FOURIER_GROUP, FOURIER_GROUP), FOURIER_GROUP ** -0.5),
        'peer_w_q': nrm(ks[21], (L, D_MODEL, PEER_HEADS * PEER_DKEY), D_MODEL ** -0.5),
        'peer_sub_keys': nrm(ks[22], (L, PEER_HEADS, 2, PEER_KEYS, PEER_HALF), PEER_HALF ** -0.5),
        'peer_u': nrm(ks[23], (L, PEER_EXPERTS, D_MODEL), D_MODEL ** -0.5),
        'peer_v': nrm(ks[24], (L, PEER_EXPERTS, D_MODEL), 1.0),
        'final_g': 1.0 + nrm(ks[25], (D_MODEL,), 0.05),
    }


def reference(x, c, ctx, c_ctx, w_mod, b_mod, norm1_g, norm2_g, w_in, w_out,
              s5_lam_re, s5_lam_im, s5_log_dt, s5_b_re, s5_b_im, s5_c_re, s5_c_im, s5_d,
              w_glu, b_glu, w_fourier, peer_w_q, peer_sub_keys, peer_u, peer_v, final_g):
    rows = x.shape[1] // GRID_W
    x = x + grid_pos_embed(rows, D_MODEL, x.dtype)[None]
    c_act = jax.nn.silu(c)
    cc_act = jax.nn.silu(c_ctx)
    for l in range(DEPTH):
        x, ctx = trunk_layer(x, ctx, c_act, cc_act, w_mod[l], b_mod[l], norm1_g[l], norm2_g[l],
                             w_in[l], w_out[l], s5_lam_re[l], s5_lam_im[l], s5_log_dt[l],
                             s5_b_re[l], s5_b_im[l], s5_c_re[l], s5_c_im[l], s5_d[l],
                             w_glu[l], b_glu[l], w_fourier[l], peer_w_q[l], peer_sub_keys[l],
                             peer_u[l], peer_v[l], update_ctx=(l < DEPTH - 1))
    return rmsnorm(x, final_g)
```

```python
import functools
import math

import numpy as np
import jax
import jax.numpy as jnp
from jax import lax
from jax.experimental import pallas as pl
from jax.experimental.pallas import tpu as pltpu

F32 = jnp.float32
BF16 = jnp.bfloat16
HIGHEST = lax.Precision.HIGHEST

GRID_W = 64
POS_BASE = 10000.0
RMS_EPS = 1e-6
N_MOD = 6
S5_GROUP = 16
S5_STATE = 64
CHUNK = 16
FOURIER_GROUPS = 4
PEER_HEADS = 8
PEER_KEYS = 128
PEER_TOPK = 16
LANES = 128
VMEM_LIMIT = 56 * 1024 * 1024


def _cparams(*sem):
    return pltpu.CompilerParams(dimension_semantics=sem, vmem_limit_bytes=VMEM_LIMIT)


def _pos_embed(rows, dim):
    quarter = dim // 4
    omega = 1.0 / (POS_BASE ** (np.arange(quarter, dtype=np.float32) / quarter))
    r = np.repeat(np.arange(rows, dtype=np.float32), GRID_W)[:, None] * omega
    col = np.tile(np.arange(GRID_W, dtype=np.float32), rows)[:, None] * omega
    emb = np.concatenate([np.sin(r), np.cos(r), np.sin(col), np.cos(col)], axis=-1)
    return emb.astype(np.float32)


def _dft_mats(n):
    k = np.arange(n, dtype=np.int64)
    ang = 2.0 * np.pi * ((k[:, None] * k[None, :]) % n).astype(np.float64) / n
    return np.cos(ang), np.sin(ang)


def _mod_kernel(c_ref, w_ref, b_ref, o_ref):
    c = c_ref[...]
    a = c * jax.nn.sigmoid(c)
    o_ref[...] = jnp.dot(a.astype(BF16), w_ref[...].astype(BF16),
                         preferred_element_type=F32) + b_ref[...]


def _modulation(c_rows, w_mod, b_mod):
    rows, d = c_rows.shape
    n = w_mod.shape[1]
    tn = d
    return pl.pallas_call(
        _mod_kernel,
        grid=(n // tn,),
        in_specs=[pl.BlockSpec((rows, d), lambda j: (0, 0)),
                  pl.BlockSpec((d, tn), lambda j: (0, j)),
                  pl.BlockSpec((1, tn), lambda j: (0, j))],
        out_specs=pl.BlockSpec((rows, tn), lambda j: (0, j)),
        out_shape=jax.ShapeDtypeStruct((rows, n), F32),
        compiler_params=_cparams("arbitrary"),
        name="modulation",
    )(c_rows, w_mod, b_mod.reshape(1, n))


def _inproj_kernel(*refs, has_pos, d, widths):
    if has_pos:
        x_ref, pos_ref, mods_ref, g_ref, w_ref = refs[:5]
        outs = refs[5:]
    else:
        x_ref, mods_ref, g_ref, w_ref = refs[:4]
        outs = refs[4:]
    x = x_ref[...]
    if has_pos:
        x = x + pos_ref[...]
    inv = lax.rsqrt(jnp.mean(x * x, axis=-1, keepdims=True) + RMS_EPS)
    m = mods_ref[...]
    sh = m[:, 0:d]
    sc = m[:, d:2 * d]
    h = (x * inv) * g_ref[...] * (1.0 + sc) + sh
    z = jnp.dot(h.astype(BF16), w_ref[...], preferred_element_type=F32)
    off = 0
    for o_ref, wd in zip(outs, widths):
        o_ref[...] = z[:, off:off + wd].astype(o_ref.dtype)
        off += wd


def _inproj(x, pos, mods3, mod_row, g, w, widths, dtypes, tl):
    bsz, length, d = x.shape
    has_pos = pos is not None
    nw = w.shape[1]
    if mod_row is None:
        mod_map = lambda b, i: (b, 0, 0)
    else:
        mod_map = lambda b, i: (mod_row, 0, 0)
    in_specs = [pl.BlockSpec((None, tl, d), lambda b, i: (b, i, 0))]
    args = [x]
    if has_pos:
        in_specs.append(pl.BlockSpec((tl, d), lambda b, i: (i, 0)))
        args.append(pos)
    in_specs += [pl.BlockSpec((None, 1, mods3.shape[2]), mod_map),
                 pl.BlockSpec((1, d), lambda b, i: (0, 0)),
                 pl.BlockSpec((d, nw), lambda b, i: (0, 0))]
    args += [mods3, g.reshape(1, d), w]
    return pl.pallas_call(
        functools.partial(_inproj_kernel, has_pos=has_pos, d=d, widths=tuple(widths)),
        grid=(bsz, length // tl),
        in_specs=in_specs,
        out_specs=[pl.BlockSpec((None, tl, wd), lambda b, i: (b, i, 0)) for wd in widths],
        out_shape=[jax.ShapeDtypeStruct((bsz, length, wd), dt) for wd, dt in zip(widths, dtypes)],
        compiler_params=_cparams("parallel", "arbitrary"),
        name="inproj_pos" if has_pos else "inproj_ctx",
    )(*args)


def _ffold_kernel(win_ref, wf_ref, cc_ref, sc_ref, oc_ref, os_ref):
    wf = wf_ref[...]
    cw = jnp.dot(cc_ref[...], wf, preferred_element_type=F32, precision=HIGHEST)
    sw = jnp.dot(sc_ref[...], wf, preferred_element_type=F32, precision=HIGHEST)
    w = win_ref[...]
    oc_ref[...] = jnp.dot(w, cw, preferred_element_type=F32, precision=HIGHEST)
    os_ref[...] = jnp.dot(w, sw, preferred_element_type=F32, precision=HIGHEST)


def _fourier_fold(w_in_f, w_fourier):
    d, fw = w_in_f.shape
    ng, gc, _ = w_fourier.shape
    cc, sc = _dft_mats(gc)
    cc = jnp.asarray(cc, F32)
    sc = jnp.asarray(sc, F32)
    return pl.pallas_call(
        _ffold_kernel,
        grid=(ng,),
        in_specs=[pl.BlockSpec((d, gc), lambda g: (0, g)),
                  pl.BlockSpec((None, gc, gc), lambda g: (g, 0, 0)),
                  pl.BlockSpec((gc, gc), lambda g: (0, 0)),
                  pl.BlockSpec((gc, gc), lambda g: (0, 0))],
        out_specs=[pl.BlockSpec((d, gc), lambda g: (0, g)),
                   pl.BlockSpec((d, gc), lambda g: (0, g))],
        out_shape=[jax.ShapeDtypeStruct((d, fw), F32), jax.ShapeDtypeStruct((d, fw), F32)],
        compiler_params=_cparams("arbitrary"),
        name="fourier_fold",
    )(w_in_f, w_fourier, cc, sc)


def _s5prep_kernel(lr_ref, li_ref, ldt_ref, btr_ref, bti_ref, cr_ref, ci_ref,
                   wm_ref, wp_ref, qt_ref, a16_ref,
                   pwr_s, pwi_s, cef_re, cef_im, ceb_re, ceb_im):
    t = CHUNK
    gp = S5_GROUP
    ns = S5_STATE
    erow = lax.broadcasted_iota(jnp.int32, (24, LANES), 0).astype(F32)
    lane = lax.broadcasted_iota(jnp.int32, (1, LANES), 1)
    gmask = [(lane < ns).astype(F32), (lane >= ns).astype(F32)]
    bbt = []
    for dr in range(2):
        lr = lr_ref[dr]
        li = li_ref[dr]
        dt = jnp.exp(ldt_ref[dr])
        mag = jnp.exp(lr * dt * erow)
        ang = li * dt * erow
        pwr_s[dr] = mag * jnp.cos(ang)
        pwi_s[dr] = mag * jnp.sin(ang)
        ar = pwr_s[dr, 1:2, :]
        ai = pwi_s[dr, 1:2, :]
        den = lr * lr + li * li
        cr = ((ar - 1.0) * lr + ai * li) / den
        ci = (ai * lr - (ar - 1.0) * li) / den
        btr = btr_ref[dr]
        bti = bti_ref[dr]
        bbt.append((cr * btr - ci * bti, cr * bti + ci * btr))
    for blk in range(t + 1):
        ef = blk
        eb = t - blk
        pr = pwr_s[0, ef:ef + 1, :]
        pi = pwi_s[0, ef:ef + 1, :]
        cef_re[blk * gp:(blk + 1) * gp, :] = cr_ref[0] * pr - ci_ref[0] * pi
        cef_im[blk * gp:(blk + 1) * gp, :] = -(cr_ref[0] * pi + ci_ref[0] * pr)
        pr = pwr_s[1, eb:eb + 1, :]
        pi = pwi_s[1, eb:eb + 1, :]
        ceb_re[blk * gp:(blk + 1) * gp, :] = cr_ref[1] * pr - ci_ref[1] * pi
        ceb_im[blk * gp:(blk + 1) * gp, :] = -(cr_ref[1] * pi + ci_ref[1] * pr)
    w = t * gp
    lane_w = lax.broadcasted_iota(jnp.int32, (gp, w), 1)
    nt = (((1,), (1,)), ((), ()))
    for gl in range(2):
        m = gmask[gl]
        btf = jnp.concatenate([bbt[0][0] * m, bbt[0][1] * m], axis=1)
        btb = jnp.concatenate([bbt[1][0] * m, bbt[1][1] * m], axis=1)
        cef = jnp.concatenate([cef_re[0:w, :], cef_im[0:w, :]], axis=1)
        ceb = jnp.concatenate([ceb_re[gp:gp + w, :], ceb_im[gp:gp + w, :]], axis=1)
        kf = lax.dot_general(btf, cef, nt, preferred_element_type=F32, precision=HIGHEST)
        kb = lax.dot_general(btb, ceb, nt, preferred_element_type=F32, precision=HIGHEST)
        for s in range(t):
            fwd = kf if s == 0 else pltpu.roll(kf, gp * s, 1)
            fwd = jnp.where(lane_w >= gp * s, fwd, 0.0)
            sh = (gp * (s + 1)) % w
            bwd = kb if sh == 0 else pltpu.roll(kb, sh, 1)
            bwd = jnp.where(lane_w < gp * (s + 1), bwd, 0.0)
            wm_ref[gl, s * gp:(s + 1) * gp, :] = (fwd + bwd).astype(wm_ref.dtype)
            ef = t - 1 - s
            pr = pwr_s[0, ef:ef + 1, :]
            pi = pwi_s[0, ef:ef + 1, :]
            r0 = gl * w + s * gp
            wp_ref[r0:r0 + gp, 0:LANES] = ((pr * bbt[0][0] - pi * bbt[0][1]) * m).astype(wp_ref.dtype)
            wp_ref[r0:r0 + gp, LANES:2 * LANES] = ((pr * bbt[0][1] + pi * bbt[0][0]) * m).astype(wp_ref.dtype)
            pr = pwr_s[1, s:s + 1, :]
            pi = pwi_s[1, s:s + 1, :]
            wp_ref[r0:r0 + gp, 2 * LANES:3 * LANES] = ((pr * bbt[1][0] - pi * bbt[1][1]) * m).astype(wp_ref.dtype)
            wp_ref[r0:r0 + gp, 3 * LANES:4 * LANES] = ((pr * bbt[1][1] + pi * bbt[1][0]) * m).astype(wp_ref.dtype)
        qt_ref[gl, :, 0:LANES] = (cef_re[gp:gp + w, :] * m).astype(qt_ref.dtype)
        qt_ref[gl, :, LANES:2 * LANES] = (cef_im[gp:gp + w, :] * m).astype(qt_ref.dtype)
        qt_ref[gl, :, 2 * LANES:3 * LANES] = (ceb_re[0:w, :] * m).astype(qt_ref.dtype)
        qt_ref[gl, :, 3 * LANES:4 * LANES] = (ceb_im[0:w, :] * m).astype(qt_ref.dtype)
    a16_ref[0:1, :] = pwr_s[0, t:t + 1, :]
    a16_ref[1:2, :] = pwi_s[0, t:t + 1, :]
    a16_ref[2:3, :] = pwr_s[1, t:t + 1, :]
    a16_ref[3:4, :] = pwi_s[1, t:t + 1, :]


def _s5_prep(lam_re, lam_im, log_dt, b_re, b_im, c_re, c_im):
    _, g, n = lam_re.shape
    p = b_re.shape[-1]
    npair = g // 2
    lanes = 2 * n
    lr = lam_re.reshape(2, npair, 1, lanes)
    li = lam_im.reshape(2, npair, 1, lanes)
    ldt = jnp.repeat(log_dt, n, axis=-1).reshape(2, npair, 1, lanes)

    def pair_rows(a):
        r = a.shape[2]
        return a.reshape(2, npair, 2, r, n).transpose(0, 1, 3, 2, 4).reshape(2, npair, r, lanes)

    btr = pair_rows(jnp.swapaxes(b_re, -1, -2))
    bti = pair_rows(jnp.swapaxes(b_im, -1, -2))
    cr = pair_rows(c_re)
    ci = pair_rows(c_im)
    w = CHUNK * p
    vec = pl.BlockSpec((2, None, 1, lanes), lambda i: (0, i, 0, 0))
    mat = pl.BlockSpec((2, None, p, lanes), lambda i: (0, i, 0, 0))
    return pl.pallas_call(
        _s5prep_kernel,
        grid=(npair,),
        in_specs=[vec, vec, vec, mat, mat, mat, mat],
        out_specs=[pl.BlockSpec((None, 2, w, w), lambda i: (i, 0, 0, 0)),
                   pl.BlockSpec((None, 2 * w, 4 * lanes), lambda i: (i, 0, 0)),
                   pl.BlockSpec((None, 2, w, 4 * lanes), lambda i: (i, 0, 0, 0)),
                   pl.BlockSpec((None, 4, lanes), lambda i: (i, 0, 0))],
        out_shape=[jax.ShapeDtypeStruct((npair, 2, w, w), BF16),
                   jax.ShapeDtypeStruct((npair, 2 * w, 4 * lanes), BF16),
                   jax.ShapeDtypeStruct((npair, 2, w, 4 * lanes), BF16),
                   jax.ShapeDtypeStruct((npair, 4, lanes), F32)],
        scratch_shapes=[pltpu.VMEM((2, 24, lanes), F32), pltpu.VMEM((2, 24, lanes), F32),
                        pltpu.VMEM((w + p, lanes), F32), pltpu.VMEM((w + p, lanes), F32),
                        pltpu.VMEM((w + p, lanes), F32), pltpu.VMEM((w + p, lanes), F32)],
        compiler_params=_cparams("arbitrary"),
        name="s5_prep",
    )(lr, li, ldt, btr, bti, cr, ci)


def _s5_kernel(u_ref, uc_ref, wm_ref, wp_ref, qt_ref, a16_ref, y_ref, st_s, stc_s, fg_s,
               *, bsz, n_chunk, n_chunk_ctx):
    w = CHUNK * S5_GROUP
    u = u_ref[...]
    wp = wp_ref[...]
    stc_s[...] = jnp.dot(uc_ref[...], wp, preferred_element_type=F32)
    st_s[...] = jnp.dot(u, wp, preferred_element_type=F32)
    a = a16_ref[...]
    afr = jnp.broadcast_to(a[0:1, :], (bsz, LANES))
    afi = jnp.broadcast_to(a[1:2, :], (bsz, LANES))
    abr = jnp.broadcast_to(a[2:3, :], (bsz, LANES))
    abi = jnp.broadcast_to(a[3:4, :], (bsz, LANES))
    zero = jnp.zeros((bsz, LANES), F32)

    def step(ar, ai, hr, hi, br, bi):
        return ar * hr - ai * hi + br, ar * hi + ai * hr + bi

    fr, fi = zero, zero
    for c in range(n_chunk_ctx):
        blk = stc_s[c * bsz:(c + 1) * bsz, :]
        fr, fi = step(afr, afi, fr, fi, blk[:, 0:LANES], blk[:, LANES:2 * LANES])
    gr, gi = zero, zero
    for c in reversed(range(n_chunk_ctx)):
        blk = stc_s[c * bsz:(c + 1) * bsz, :]
        gr, gi = step(abr, abi, gr, gi, blk[:, 2 * LANES:3 * LANES], blk[:, 3 * LANES:4 * LANES])

    def fwd_body(c, carry):
        hr, hi = carry
        r0 = pl.multiple_of(c * bsz, bsz)
        fg_s[pl.ds(r0, bsz), 0:LANES] = hr
        fg_s[pl.ds(r0, bsz), LANES:2 * LANES] = hi
        return step(afr, afi, hr, hi, st_s[pl.ds(r0, bsz), 0:LANES], st_s[pl.ds(r0, bsz), LANES:2 * LANES])

    lax.fori_loop(0, n_chunk, fwd_body, (fr, fi))

    def bwd_body(k, carry):
        hr, hi = carry
        c = n_chunk - 1 - k
        r0 = pl.multiple_of(c * bsz, bsz)
        fg_s[pl.ds(r0, bsz), 2 * LANES:3 * LANES] = hr
        fg_s[pl.ds(r0, bsz), 3 * LANES:4 * LANES] = hi
        return step(abr, abi, hr, hi, st_s[pl.ds(r0, bsz), 2 * LANES:3 * LANES],
                    st_s[pl.ds(r0, bsz), 3 * LANES:4 * LANES])

    lax.fori_loop(0, n_chunk, bwd_body, (gr, gi))

    fg = fg_s[...].astype(BF16)
    nt = (((1,), (1,)), ((), ()))
    for gl in range(2):
        intra = jnp.dot(u[:, gl * w:(gl + 1) * w], wm_ref[gl], preferred_element_type=F32)
        inter = lax.dot_general(fg, qt_ref[gl], nt, preferred_element_type=F32)
        y_ref[:, gl * w:(gl + 1) * w] = intra + inter


def _s5_scan(u2, uc2, wm, wp, qt, a16, bsz):
    rows, width = u2.shape
    rows_c = uc2.shape[0]
    npair = wm.shape[0]
    w = CHUNK * S5_GROUP
    pw = 2 * w
    return pl.pallas_call(
        functools.partial(_s5_kernel, bsz=bsz, n_chunk=rows // bsz, n_chunk_ctx=rows_c // bsz),
        grid=(npair,),
        in_specs=[pl.BlockSpec((rows, pw), lambda i: (0, i)),
                  pl.BlockSpec((rows_c, pw), lambda i: (0, i)),
                  pl.BlockSpec((None, 2, w, w), lambda i: (i, 0, 0, 0)),
                  pl.BlockSpec((None, pw, 4 * LANES), lambda i: (i, 0, 0)),
                  pl.BlockSpec((None, 2, w, 4 * LANES), lambda i: (i, 0, 0, 0)),
                  pl.BlockSpec((None, 4, LANES), lambda i: (i, 0, 0))],
        out_specs=pl.BlockSpec((rows, pw), lambda i: (0, i)),
        out_shape=jax.ShapeDtypeStruct((rows, width), F32),
        scratch_shapes=[pltpu.VMEM((rows, 4 * LANES), F32), pltpu.VMEM((rows_c, 4 * LANES), F32),
                        pltpu.VMEM((rows, 4 * LANES), F32)],
        compiler_params=_cparams("parallel"),
        name="s5_scan",
    )(u2, uc2, wm, wp, qt, a16)


def _to_chunks(u, bsz, length):
    g = u.shape[-1] // S5_GROUP
    nc = length // CHUNK
    v = u.reshape(bsz, nc, CHUNK, g, S5_GROUP).transpose(1, 0, 3, 2, 4)
    return v.reshape(nc * bsz, g * CHUNK * S5_GROUP)


def _from_chunks(y2, bsz, length):
    g = y2.shape[-1] // (CHUNK * S5_GROUP)
    nc = length // CHUNK
    v = y2.reshape(nc, bsz, g, CHUNK, S5_GROUP).transpose(1, 0, 3, 2, 4)
    return v.reshape(bsz, length, g * S5_GROUP)


def _dft_kernel(c_ref, s_ref, gc_ref, gs_ref, o_ref, *, scale):
    acc = jnp.dot(c_ref[...], gc_ref[...], preferred_element_type=F32)
    acc = acc - jnp.dot(s_ref[...], gs_ref[...], preferred_element_type=F32)
    o_ref[...] = (acc * scale).astype(o_ref.dtype)


def _seq_dft(gc, gs, scale, tk):
    bsz, length, fw = gc.shape
    cm, sm = _dft_mats(length)
    cm = jnp.asarray(cm, BF16)
    sm = jnp.asarray(sm, BF16)
    return pl.pallas_call(
        functools.partial(_dft_kernel, scale=scale),
        grid=(bsz, length // tk),
        in_specs=[pl.BlockSpec((tk, length), lambda b, k: (k, 0)),
                  pl.BlockSpec((tk, length), lambda b, k: (k, 0)),
                  pl.BlockSpec((None, length, fw), lambda b, k: (b, 0, 0)),
                  pl.BlockSpec((None, length, fw), lambda b, k: (b, 0, 0))],
        out_specs=pl.BlockSpec((None, tk, fw), lambda b, k: (b, k, 0)),
        out_shape=jax.ShapeDtypeStruct((bsz, length, fw), BF16),
        compiler_params=_cparams("parallel", "arbitrary"),
        name="seq_dft",
    )(cm, sm, gc, gs)


def _outproj_kernel(y_ref, u_ref, fo_ref, x_ref, pos_ref, mods_ref, dsk_ref, wg_ref, bg_ref,
                    wo1_ref, wo2_ref, g2_ref, x1_ref, h2_ref, *, d):
    u = u_ref[...]
    y = y_ref[...] + dsk_ref[...] * u
    y = jax.nn.gelu(y)
    gate = jax.nn.sigmoid(jnp.dot(y.astype(BF16), wg_ref[...], preferred_element_type=F32) + bg_ref[...])
    s5o = (y * gate).astype(BF16)
    mix = jnp.dot(s5o, wo1_ref[...], preferred_element_type=F32)
    mix = mix + jnp.dot(fo_ref[...], wo2_ref[...], preferred_element_type=F32)
    m = mods_ref[...]
    g1 = m[:, 2 * d:3 * d]
    sh2 = m[:, 3 * d:4 * d]
    sc2 = m[:, 4 * d:5 * d]
    x1 = x_ref[...] + pos_ref[...] + g1 * mix
    x1_ref[...] = x1
    inv = lax.rsqrt(jnp.mean(x1 * x1, axis=-1, keepdims=True) + RMS_EPS)
    h2_ref[...] = ((x1 * inv) * g2_ref[...] * (1.0 + sc2) + sh2).astype(h2_ref.dtype)


def _outproj(y, u, fo, x, pos, mods3, d_skip, w_glu, b_glu, w_out, norm2_g, tl):
    bsz, length, d = x.shape
    sw = y.shape[-1]
    fw = fo.shape[-1]
    tok = lambda wd: pl.BlockSpec((None, tl, wd), lambda b, i: (b, i, 0))
    full = lambda r, c: pl.BlockSpec((r, c), lambda b, i: (0, 0))
    return pl.pallas_call(
        functools.partial(_outproj_kernel, d=d),
        grid=(bsz, length // tl),
        in_specs=[tok(sw), tok(sw), tok(fw), tok(d),
                  pl.BlockSpec((tl, d), lambda b, i: (i, 0)),
                  pl.BlockSpec((None, 1, mods3.shape[2]), lambda b, i: (b, 0, 0)),
                  full(1, sw), full(sw, sw), full(1, sw), full(sw, d), full(fw, d), full(1, d)],
        out_specs=[tok(d), tok(d)],
        out_shape=[jax.ShapeDtypeStruct((bsz, length, d), F32),
                   jax.ShapeDtypeStruct((bsz, length, d), BF16)],
        compiler_params=_cparams("parallel", "arbitrary"),
        name="outproj",
    )(y, u, fo, x, pos, mods3, d_skip.reshape(1, sw), w_glu.astype(BF16), b_glu.reshape(1, sw),
      w_out[:sw].astype(BF16), w_out[sw:].astype(BF16), norm2_g.reshape(1, d))


def _wqs_kernel(sk_ref, wq_ref, o_ref):
    nt = (((1,), (1,)), ((), ()))
    o_ref[...] = lax.dot_general(sk_ref[...], wq_ref[...], nt,
                                 preferred_element_type=F32, precision=HIGHEST)


def _fold_query_keys(w_q, sub_keys):
    d = w_q.shape[0]
    nh, _, nk, half = sub_keys.shape
    units = 2 * nh
    out = pl.pallas_call(
        _wqs_kernel,
        grid=(units,),
        in_specs=[pl.BlockSpec((None, None, nk, half), lambda u: (u % nh, u // nh, 0, 0)),
                  pl.BlockSpec((d, half), lambda u: (0, (u % nh) * 2 + u // nh))],
        out_specs=pl.BlockSpec((None, nk, d), lambda u: (u, 0, 0)),
        out_shape=jax.ShapeDtypeStruct((units, nk, d), F32),
        compiler_params=_cparams("arbitrary"),
        name="fold_query_keys",
    )(sub_keys, w_q)
    return out.transpose(1, 0, 2).reshape(nk * units, d).astype(BF16)


def _oddeven_merge_sort_pairs(n):
    pairs = []
    p = 1
    while p < n:
        k = p
        while k >= 1:
            for j in range(k % p, n - k, 2 * k):
                for i in range(min(k, n - j - k)):
                    if (i + j) // (2 * p) == (i + j + k) // (2 * p):
                        pairs.append((i + j, i + j + k))
            k //= 2
        p *= 2
    return pairs


_SORT16 = _oddeven_merge_sort_pairs(PEER_TOPK)


def _sort_desc(v):
    v = list(v)
    for i, j in _SORT16:
        hi = jnp.maximum(v[i], v[j])
        lo = jnp.minimum(v[i], v[j])
        v[i], v[j] = hi, lo
    return v


def _merge_top(a, b):
    n = len(a)
    c = [jnp.maximum(a[i], b[n - 1 - i]) for i in range(n)]
    k = n // 2
    while k >= 1:
        for i in range(n):
            if (i & k) == 0:
                hi = jnp.maximum(c[i], c[i + k])
                lo = jnp.minimum(c[i], c[i + k])
                c[i], c[i + k] = hi, lo
        k //= 2
    return c


def _top_sorted(vals):
    k = PEER_TOPK
    runs = [_sort_desc(vals[i:i + k]) for i in range(0, len(vals), k)]
    while len(runs) > 1:
        runs = [_merge_top(runs[i], runs[i + 1]) for i in range(0, len(runs), 2)]
    return runs[0]


def _cand_cells():
    k = PEER_TOPK
    return [(r0, r1) for r0 in range(k) for r1 in range(k) if (r0 + 1) * (r1 + 1) <= k]


_CELLS = _cand_cells()


def _select_block(s0, s1):
    k = PEER_TOPK
    shape = s0[0].shape
    one = jnp.ones(shape, F32)
    zero = jnp.zeros(shape, F32)
    sv0 = _top_sorted(s0)
    sv1 = _top_sorted(s1)
    cand = {c: sv0[c[0]] + sv1[c[1]] for c in _CELLS}
    cnt = {}
    for c in _CELLS:
        cnt[c] = float(sum(1 for o in _CELLS if o != c and o[0] <= c[0] and o[1] <= c[1]))
    acc = {c: None for c in _CELLS}
    for a_i, ca in enumerate(_CELLS):
        for cb in _CELLS[a_i + 1:]:
            if ca[0] <= cb[0] and ca[1] <= cb[1]:
                continue
            first = cand[ca] >= cand[cb]
            ib = jnp.where(first, one, zero)
            acc[cb] = ib if acc[cb] is None else acc[cb] + ib
            ia = one - ib
            acc[ca] = ia if acc[ca] is None else acc[ca] + ia
    e0 = [jnp.exp(sv0[r] - sv0[0]) for r in range(k)]
    e1 = [jnp.exp(sv1[r] - sv1[0]) for r in range(k)]
    nsel = [zero] * k
    zsum = zero
    for c in _CELLS:
        tot = acc[c] + cnt[c] if acc[c] is not None else jnp.full(shape, cnt[c], F32)
        sel = jnp.where(tot < float(k), one, zero)
        nsel[c[0]] = nsel[c[0]] + sel
        zsum = zsum + sel * (e0[c[0]] * e1[c[1]])
    inv_z = 1.0 / zsum
    n_l, ex0_l, r_l, ex1_l = [], [], [], []
    for e in range(len(s0)):
        n_e = zero
        r_e = jnp.full(shape, float(k), F32)
        for r in reversed(range(k)):
            n_e = jnp.where(s0[e] >= sv0[r], nsel[r], n_e)
            r_e = jnp.where(s1[e] >= sv1[r], float(r), r_e)
        n_l.append(n_e)
        r_l.append(r_e)
        ex0_l.append(jnp.exp(s0[e] - sv0[0]) * inv_z)
        ex1_l.append(jnp.exp(s1[e] - sv1[0]))
    return n_l, ex0_l, r_l, ex1_l


def _select_kernel(wqs_ref, h2_ref, n0_ref, ex0_ref, r1_ref, ex1_ref, s_s, *, tm):
    nh = PEER_HEADS
    nk = PEER_KEYS
    units = 2 * nh
    nt = (((1,), (1,)), ((), ()))
    s_s[...] = lax.dot_general(wqs_ref[...], h2_ref[...], nt, preferred_element_type=F32)

    def block(j, carry):
        c0 = pl.multiple_of(j * LANES, LANES)
        cols = pl.ds(c0, LANES)
        s0 = [s_s[e * units:e * units + nh, cols] for e in range(nk)]
        s1 = [s_s[e * units + nh:(e + 1) * units, cols] for e in range(nk)]
        n_l, ex0_l, r_l, ex1_l = _select_block(s0, s1)
        for e in range(nk):
            n0_ref[e, :, cols] = n_l[e]
            r1_ref[e, :, cols] = r_l[e]
            ex0_ref[e, :, cols] = ex0_l[e]
            ex1_ref[e, :, cols] = ex1_l[e]
        return carry

    lax.fori_loop(0, tm // LANES, block, 0)


def _peer_select(wqs, h2, tm):
    tokens, d = h2.shape
    nh = PEER_HEADS
    nk = PEER_KEYS
    rows = wqs.shape[0]
    o_spec = pl.BlockSpec((nk, nh, tm), lambda i: (0, 0, i))
    o_shape = jax.ShapeDtypeStruct((nk, nh, tokens), F32)
    return pl.pallas_call(
        functools.partial(_select_kernel, tm=tm),
        grid=(tokens // tm,),
        in_specs=[pl.BlockSpec((rows, d), lambda i: (0, 0)),
                  pl.BlockSpec((tm, d), lambda i: (i, 0))],
        out_specs=[o_spec, o_spec, o_spec, o_spec],
        out_shape=[o_shape, o_shape, o_shape, o_shape],
        scratch_shapes=[pltpu.VMEM((rows, tm), F32)],
        compiler_params=_cparams("parallel"),
        name="peer_select",
    )(wqs, h2)


def _experts_kernel(u_ref, vt_ref, h2_ref, n0_ref, ex0_ref, r1_ref, ex1_ref, x1_ref, mods_ref, fg_ref,
                    o_ref, a_s, wa_s, acc_s, r1_s, ex1_s, *, d, te, tm):
    nh = PEER_HEADS
    nk = PEER_KEYS
    j = pl.program_id(1)
    nt = (((1,), (1,)), ((), ()))

    @pl.when(j == 0)
    def _():
        acc_s[...] = jnp.zeros_like(acc_s)
        for h in range(nh):
            r1_s[h] = r1_ref[:, h, :].astype(BF16)
            ex1_s[h] = ex1_ref[:, h, :].astype(BF16)

    a_s[...] = lax.dot_general(u_ref[...], h2_ref[...], nt, preferred_element_type=F32)

    def slab(el, carry):
        r0 = pl.multiple_of(el * nk, nk)
        n0 = n0_ref[el].astype(BF16)
        ex0 = ex0_ref[el].astype(BF16)
        wgt = jnp.zeros((nk, tm), BF16)
        for h in range(nh):
            keep = r1_s[h] < n0[h:h + 1, :]
            wgt = wgt + jnp.where(keep, ex1_s[h], jnp.zeros_like(ex1_s[h])) * ex0[h:h + 1, :]
        act = jax.nn.gelu(a_s[pl.ds(r0, nk), :])
        wa_s[pl.ds(r0, nk), :] = (act * wgt.astype(F32)).astype(BF16)
        return carry

    lax.fori_loop(0, te // nk, slab, 0)
    acc_s[...] += jnp.dot(vt_ref[...], wa_s[...], preferred_element_type=F32)

    @pl.when(j == pl.num_programs(1) - 1)
    def _():
        peer = acc_s[...].T
        m = mods_ref[...]
        g2 = m[:, 5 * d:6 * d]
        xo = x1_ref[...] + g2 * peer
        inv = lax.rsqrt(jnp.mean(xo * xo, axis=-1, keepdims=True) + RMS_EPS)
        o_ref[...] = (xo * inv) * fg_ref[...]


def _peer_experts(u_bf, vt_bf, h2, n0, ex0, r1, ex1, x1, mods3, final_g, tm, te):
    bsz, length, d = x1.shape
    tokens = bsz * length
    ne = u_bf.shape[0]
    nh = PEER_HEADS
    nk = PEER_KEYS
    tpb = length // tm
    sel0 = pl.BlockSpec((te // nk, nh, tm), lambda i, j: (j, 0, i))
    sel1 = pl.BlockSpec((nk, nh, tm), lambda i, j: (0, 0, i))
    return pl.pallas_call(
        functools.partial(_experts_kernel, d=d, te=te, tm=tm),
        grid=(tokens // tm, ne // te),
        in_specs=[pl.BlockSpec((te, d), lambda i, j: (j, 0)),
                  pl.BlockSpec((d, te), lambda i, j: (0, j)),
                  pl.BlockSpec((tm, d), lambda i, j: (i, 0)),
                  sel0, sel0, sel1, sel1,
                  pl.BlockSpec((None, tm, d), lambda i, j: (i // tpb, i % tpb, 0)),
                  pl.BlockSpec((None, 1, mods3.shape[2]), lambda i, j: (i // tpb, 0, 0)),
                  pl.BlockSpec((1, d), lambda i, j: (0, 0))],
        out_specs=pl.BlockSpec((None, tm, d), lambda i, j: (i // tpb, i % tpb, 0)),
        out_shape=jax.ShapeDtypeStruct((bsz, length, d), F32),
        scratch_shapes=[pltpu.VMEM((te, tm), F32), pltpu.VMEM((te, tm), BF16),
                        pltpu.VMEM((d, tm), F32),
                        pltpu.VMEM((nh, nk, tm), BF16), pltpu.VMEM((nh, nk, tm), BF16)],
        compiler_params=_cparams("parallel", "arbitrary"),
        name="peer_experts",
    )(u_bf, vt_bf, h2, n0, ex0, r1, ex1, x1, mods3, final_g.reshape(1, d))


def kernel(x, c, ctx, c_ctx, w_mod, b_mod, norm1_g, norm2_g, w_in, w_out, s5_lam_re, s5_lam_im,
           s5_log_dt, s5_b_re, s5_b_im, s5_c_re, s5_c_im, s5_d, w_glu, b_glu, w_fourier,
           peer_w_q, peer_sub_keys, peer_u, peer_v, final_g):
    bsz, length, d = x.shape
    clen = ctx.shape[1]
    sw = s5_d.shape[-1]
    fw = w_in.shape[-1] - sw
    assert w_mod.shape[0] == 1, "single layer: the context stream only feeds the S5 initial states"
    assert length % CHUNK == 0 and clen % CHUNK == 0 and bsz % 8 == 0

    tl = min(512, length)
    tlc = min(512, clen)
    tm = min(512, length)
    te = 1024

    pos = jnp.asarray(_pos_embed(length // GRID_W, d))

    mrows = -(-(bsz + 1) // 8) * 8
    c_rows = jnp.concatenate([c, c_ctx[None, :], jnp.zeros((mrows - bsz - 1, d), F32)], axis=0)
    mods = _modulation(c_rows, w_mod[0], b_mod[0])
    mods3 = mods.reshape(mrows, 1, N_MOD * d)

    wfc, wfs = _fourier_fold(w_in[0][:, sw:], w_fourier[0])
    w_all = jnp.concatenate([w_in[0][:, :sw], wfc, wfs], axis=1).astype(BF16)

    u, gc, gs = _inproj(x, pos, mods3, None, norm1_g[0], w_all, (sw, fw, fw), (F32, BF16, BF16), tl)
    (uc,) = _inproj(ctx, None, mods3, bsz, norm1_g[0], w_all[:, :sw], (sw,), (F32,), tlc)

    wm, wp, qt, a16 = _s5_prep(s5_lam_re[0], s5_lam_im[0], s5_log_dt[0], s5_b_re[0], s5_b_im[0],
                               s5_c_re[0], s5_c_im[0])
    u2 = _to_chunks(u, bsz, length).astype(BF16)
    uc2 = _to_chunks(uc, bsz, clen).astype(BF16)
    y2 = _s5_scan(u2, uc2, wm, wp, qt, a16, bsz)
    y = _from_chunks(y2, bsz, length)

    fo = _seq_dft(gc, gs, 1.0 / math.sqrt(length * (fw // FOURIER_GROUPS)), min(512, length))

    x1, h2 = _outproj(y, u, fo, x, pos, mods3, s5_d[0], w_glu[0], b_glu[0], w_out[0], norm2_g[0], tl)

    wqs = _fold_query_keys(peer_w_q[0], peer_sub_keys[0])
    h2f = h2.reshape(bsz * length, d)
    n0, ex0, r1, ex1 = _peer_select(wqs, h2f, tm)
    u_bf = peer_u[0].astype(BF16)
    vt_bf = peer_v[0].astype(BF16).T
    return _peer_experts(u_bf, vt_bf, h2f, n0, ex0, r1, ex1, x1, mods3, final_g, tm, te)
```

```python
import functools
import math

import numpy as np
import jax
import jax.numpy as jnp
from jax import lax
from jax.experimental import pallas as pl
from jax.experimental.pallas import tpu as pltpu

F32 = jnp.float32
BF16 = jnp.bfloat16
HIGHEST = lax.Precision.HIGHEST

GRID_W = 64
POS_BASE = 10000.0
RMS_EPS = 1e-6
N_MOD = 6
S5_GROUP = 16
S5_STATE = 64
CHUNK = 16
FOURIER_GROUPS = 4
PEER_HEADS = 8
PEER_KEYS = 128
PEER_TOPK = 16
LANES = 128
VMEM_LIMIT = 56 * 1024 * 1024


def _cparams(*sem):
    return pltpu.CompilerParams(dimension_semantics=sem, vmem_limit_bytes=VMEM_LIMIT)


def _pos_embed(rows, dim):
    quarter = dim // 4
    omega = 1.0 / (POS_BASE ** (np.arange(quarter, dtype=np.float32) / quarter))
    r = np.repeat(np.arange(rows, dtype=np.float32), GRID_W)[:, None] * omega
    col = np.tile(np.arange(GRID_W, dtype=np.float32), rows)[:, None] * omega
    emb = np.concatenate([np.sin(r), np.cos(r), np.sin(col), np.cos(col)], axis=-1)
    return emb.astype(np.float32)


def _dft_mats(n):
    k = np.arange(n, dtype=np.int64)
    ang = 2.0 * np.pi * ((k[:, None] * k[None, :]) % n).astype(np.float64) / n
    return np.cos(ang), np.sin(ang)


def _mod_kernel(c_ref, w_ref, b_ref, o_ref):
    c = c_ref[...]
    a = c * jax.nn.sigmoid(c)
    o_ref[...] = jnp.dot(a.astype(BF16), w_ref[...].astype(BF16),
                         preferred_element_type=F32) + b_ref[...]


def _modulation(c_rows, w_mod, b_mod):
    rows, d = c_rows.shape
    n = w_mod.shape[1]
    tn = d
    return pl.pallas_call(
        _mod_kernel,
        grid=(n // tn,),
        in_specs=[pl.BlockSpec((rows, d), lambda j: (0, 0)),
                  pl.BlockSpec((d, tn), lambda j: (0, j)),
                  pl.BlockSpec((1, tn), lambda j: (0, j))],
        out_specs=pl.BlockSpec((rows, tn), lambda j: (0, j)),
        out_shape=jax.ShapeDtypeStruct((rows, n), F32),
        compiler_params=_cparams("arbitrary"),
        name="modulation",
    )(c_rows, w_mod, b_mod.reshape(1, n))


def _inproj_kernel(*refs, has_pos, d, widths):
    if has_pos:
        x_ref, pos_ref, mods_ref, g_ref, w_ref = refs[:5]
        outs = refs[5:]
    else:
        x_ref, mods_ref, g_ref, w_ref = refs[:4]
        outs = refs[4:]
    x = x_ref[...]
    if has_pos:
        x = x + pos_ref[...]
    inv = lax.rsqrt(jnp.mean(x * x, axis=-1, keepdims=True) + RMS_EPS)
    m = mods_ref[...]
    sh = m[:, 0:d]
    sc = m[:, d:2 * d]
    h = (x * inv) * g_ref[...] * (1.0 + sc) + sh
    z = jnp.dot(h.astype(BF16), w_ref[...], preferred_element_type=F32)
    off = 0
    for o_ref, wd in zip(outs, widths):
        o_ref[...] = z[:, off:off + wd].astype(o_ref.dtype)
        off += wd


def _inproj(x, pos, mods3, mod_row, g, w, widths, dtypes, tl):
    bsz, length, d = x.shape
    has_pos = pos is not None
    nw = w.shape[1]
    if mod_row is None:
        mod_map = lambda b, i: (b, 0, 0)
    else:
        mod_map = lambda b, i: (mod_row, 0, 0)
    in_specs = [pl.BlockSpec((None, tl, d), lambda b, i: (b, i, 0))]
    args = [x]
    if has_pos:
        in_specs.append(pl.BlockSpec((tl, d), lambda b, i: (i, 0)))
        args.append(pos)
    in_specs += [pl.BlockSpec((None, 1, mods3.shape[2]), mod_map),
                 pl.BlockSpec((1, d), lambda b, i: (0, 0)),
                 pl.BlockSpec((d, nw), lambda b, i: (0, 0))]
    args += [mods3, g.reshape(1, d), w]
    return pl.pallas_call(
        functools.partial(_inproj_kernel, has_pos=has_pos, d=d, widths=tuple(widths)),
        grid=(bsz, length // tl),
        in_specs=in_specs,
        out_specs=[pl.BlockSpec((None, tl, wd), lambda b, i: (b, i, 0)) for wd in widths],
        out_shape=[jax.ShapeDtypeStruct((bsz, length, wd), dt) for wd, dt in zip(widths, dtypes)],
        compiler_params=_cparams("parallel", "arbitrary"),
        name="inproj_pos" if has_pos else "inproj_ctx",
    )(*args)


def _ffold_kernel(win_ref, wf_ref, cc_ref, sc_ref, oc_ref, os_ref):
    wf = wf_ref[...]
    cw = jnp.dot(cc_ref[...], wf, preferred_element_type=F32, precision=HIGHEST)
    sw = jnp.dot(sc_ref[...], wf, preferred_element_type=F32, precision=HIGHEST)
    w = win_ref[...]
    oc_ref[...] = jnp.dot(w, cw, preferred_element_type=F32, precision=HIGHEST)
    os_ref[...] = jnp.dot(w, sw, preferred_element_type=F32, precision=HIGHEST)


def _fourier_fold(w_in_f, w_fourier):
    d, fw = w_in_f.shape
    ng, gc, _ = w_fourier.shape
    cc, sc = _dft_mats(gc)
    cc = jnp.asarray(cc, F32)
    sc = jnp.asarray(sc, F32)
    return pl.pallas_call(
        _ffold_kernel,
        grid=(ng,),
        in_specs=[pl.BlockSpec((d, gc), lambda g: (0, g)),
                  pl.BlockSpec((None, gc, gc), lambda g: (g, 0, 0)),
                  pl.BlockSpec((gc, gc), lambda g: (0, 0)),
                  pl.BlockSpec((gc, gc), lambda g: (0, 0))],
        out_specs=[pl.BlockSpec((d, gc), lambda g: (0, g)),
                   pl.BlockSpec((d, gc), lambda g: (0, g))],
        out_shape=[jax.ShapeDtypeStruct((d, fw), F32), jax.ShapeDtypeStruct((d, fw), F32)],
        compiler_params=_cparams("arbitrary"),
        name="fourier_fold",
    )(w_in_f, w_fourier, cc, sc)


def _s5prep_kernel(lr_ref, li_ref, ldt_ref, btr_ref, bti_ref, cr_ref, ci_ref,
                   wm_ref, wp_ref, qt_ref, a16_ref,
                   pwr_s, pwi_s, cef_re, cef_im, ceb_re, ceb_im):
    t = CHUNK
    gp = S5_GROUP
    ns = S5_STATE
    erow = lax.broadcasted_iota(jnp.int32, (24, LANES), 0).astype(F32)
    lane = lax.broadcasted_iota(jnp.int32, (1, LANES), 1)
    gmask = [(lane < ns).astype(F32), (lane >= ns).astype(F32)]
    bbt = []
    for dr in range(2):
        lr = lr_ref[dr]
        li = li_ref[dr]
        dt = jnp.exp(ldt_ref[dr])
        mag = jnp.exp(lr * dt * erow)
        ang = li * dt * erow
        pwr_s[dr] = mag * jnp.cos(ang)
        pwi_s[dr] = mag * jnp.sin(ang)
        ar = pwr_s[dr, 1:2, :]
        ai = pwi_s[dr, 1:2, :]
        den = lr * lr + li * li
        cr = ((ar - 1.0) * lr + ai * li) / den
        ci = (ai * lr - (ar - 1.0) * li) / den
        btr = btr_ref[dr]
        bti = bti_ref[dr]
        bbt.append((cr * btr - ci * bti, cr * bti + ci * btr))
    for blk in range(t + 1):
        ef = blk
        eb = t - blk
        pr = pwr_s[0, ef:ef + 1, :]
        pi = pwi_s[0, ef:ef + 1, :]
        cef_re[blk * gp:(blk + 1) * gp, :] = cr_ref[0] * pr - ci_ref[0] * pi
        cef_im[blk * gp:(blk + 1) * gp, :] = -(cr_ref[0] * pi + ci_ref[0] * pr)
        pr = pwr_s[1, eb:eb + 1, :]
        pi = pwi_s[1, eb:eb + 1, :]
        ceb_re[blk * gp:(blk + 1) * gp, :] = cr_ref[1] * pr - ci_ref[1] * pi
        ceb_im[blk * gp:(blk + 1) * gp, :] = -(cr_ref[1] * pi + ci_ref[1] * pr)
    w = t * gp
    lane_w = lax.broadcasted_iota(jnp.int32, (gp, w), 1)
    nt = (((1,), (1,)), ((), ()))
    for gl in range(2):
        m = gmask[gl]
        btf = jnp.concatenate([bbt[0][0] * m, bbt[0][1] * m], axis=1)
        btb = jnp.concatenate([bbt[1][0] * m, bbt[1][1] * m], axis=1)
        cef = jnp.concatenate([cef_re[0:w, :], cef_im[0:w, :]], axis=1)
        ceb = jnp.concatenate([ceb_re[gp:gp + w, :], ceb_im[gp:gp + w, :]], axis=1)
        kf = lax.dot_general(btf, cef, nt, preferred_element_type=F32, precision=HIGHEST)
        kb = lax.dot_general(btb, ceb, nt, preferred_element_type=F32, precision=HIGHEST)
        for s in range(t):
            fwd = kf if s == 0 else pltpu.roll(kf, gp * s, 1)
            fwd = jnp.where(lane_w >= gp * s, fwd, 0.0)
            sh = (gp * (s + 1)) % w
            bwd = kb if sh == 0 else pltpu.roll(kb, sh, 1)
            bwd = jnp.where(lane_w < gp * (s + 1), bwd, 0.0)
            wm_ref[gl, s * gp:(s + 1) * gp, :] = (fwd + bwd).astype(wm_ref.dtype)
            ef = t - 1 - s
            pr = pwr_s[0, ef:ef + 1, :]
            pi = pwi_s[0, ef:ef + 1, :]
            r0 = gl * w + s * gp
            wp_ref[r0:r0 + gp, 0:LANES] = ((pr * bbt[0][0] - pi * bbt[0][1]) * m).astype(wp_ref.dtype)
            wp_ref[r0:r0 + gp, LANES:2 * LANES] = ((pr * bbt[0][1] + pi * bbt[0][0]) * m).astype(wp_ref.dtype)
            pr = pwr_s[1, s:s + 1, :]
            pi = pwi_s[1, s:s + 1, :]
            wp_ref[r0:r0 + gp, 2 * LANES:3 * LANES] = ((pr * bbt[1][0] - pi * bbt[1][1]) * m).astype(wp_ref.dtype)
            wp_ref[r0:r0 + gp, 3 * LANES:4 * LANES] = ((pr * bbt[1][1] + pi * bbt[1][0]) * m).astype(wp_ref.dtype)
        qt_ref[gl, :, 0:LANES] = (cef_re[gp:gp + w, :] * m).astype(qt_ref.dtype)
        qt_ref[gl, :, LANES:2 * LANES] = (cef_im[gp:gp + w, :] * m).astype(qt_ref.dtype)
        qt_ref[gl, :, 2 * LANES:3 * LANES] = (ceb_re[0:w, :] * m).astype(qt_ref.dtype)
        qt_ref[gl, :, 3 * LANES:4 * LANES] = (ceb_im[0:w, :] * m).astype(qt_ref.dtype)
    a16_ref[0:1, :] = pwr_s[0, t:t + 1, :]
    a16_ref[1:2, :] = pwi_s[0, t:t + 1, :]
    a16_ref[2:3, :] = pwr_s[1, t:t + 1, :]
    a16_ref[3:4, :] = pwi_s[1, t:t + 1, :]


def _s5_prep(lam_re, lam_im, log_dt, b_re, b_im, c_re, c_im):
    _, g, n = lam_re.shape
    p = b_re.shape[-1]
    npair = g // 2
    lanes = 2 * n
    lr = lam_re.reshape(2, npair, 1, lanes)
    li = lam_im.reshape(2, npair, 1, lanes)
    ldt = jnp.repeat(log_dt, n, axis=-1).reshape(2, npair, 1, lanes)

    def pair_rows(a):
        r = a.shape[2]
        return a.reshape(2, npair, 2, r, n).transpose(0, 1, 3, 2, 4).reshape(2, npair, r, lanes)

    btr = pair_rows(jnp.swapaxes(b_re, -1, -2))
    bti = pair_rows(jnp.swapaxes(b_im, -1, -2))
    cr = pair_rows(c_re)
    ci = pair_rows(c_im)
    w = CHUNK * p
    vec = pl.BlockSpec((2, None, 1, lanes), lambda i: (0, i, 0, 0))
    mat = pl.BlockSpec((2, None, p, lanes), lambda i: (0, i, 0, 0))
    return pl.pallas_call(
        _s5prep_kernel,
        grid=(npair,),
        in_specs=[vec, vec, vec, mat, mat, mat, mat],
        out_specs=[pl.BlockSpec((None, 2, w, w), lambda i: (i, 0, 0, 0)),
                   pl.BlockSpec((None, 2 * w, 4 * lanes), lambda i: (i, 0, 0)),
                   pl.BlockSpec((None, 2, w, 4 * lanes), lambda i: (i, 0, 0, 0)),
                   pl.BlockSpec((None, 4, lanes), lambda i: (i, 0, 0))],
        out_shape=[jax.ShapeDtypeStruct((npair, 2, w, w), BF16),
                   jax.ShapeDtypeStruct((npair, 2 * w, 4 * lanes), BF16),
                   jax.ShapeDtypeStruct((npair, 2, w, 4 * lanes), BF16),
                   jax.ShapeDtypeStruct((npair, 4, lanes), F32)],
        scratch_shapes=[pltpu.VMEM((2, 24, lanes), F32), pltpu.VMEM((2, 24, lanes), F32),
                        pltpu.VMEM((w + p, lanes), F32), pltpu.VMEM((w + p, lanes), F32),
                        pltpu.VMEM((w + p, lanes), F32), pltpu.VMEM((w + p, lanes), F32)],
        compiler_params=_cparams("arbitrary"),
        name="s5_prep",
    )(lr, li, ldt, btr, bti, cr, ci)


def _s5_kernel(u_ref, uc_ref, wm_ref, wp_ref, qt_ref, a16_ref, y_ref, st_s, stc_s, fg_s,
               *, bsz, n_chunk, n_chunk_ctx):
    w = CHUNK * S5_GROUP
    u = u_ref[...]
    wp = wp_ref[...]
    stc_s[...] = jnp.dot(uc_ref[...], wp, preferred_element_type=F32)
    st_s[...] = jnp.dot(u, wp, preferred_element_type=F32)
    a = a16_ref[...]
    afr = jnp.broadcast_to(a[0:1, :], (bsz, LANES))
    afi = jnp.broadcast_to(a[1:2, :], (bsz, LANES))
    abr = jnp.broadcast_to(a[2:3, :], (bsz, LANES))
    abi = jnp.broadcast_to(a[3:4, :], (bsz, LANES))
    zero = jnp.zeros((bsz, LANES), F32)

    def step(ar, ai, hr, hi, br, bi):
        return ar * hr - ai * hi + br, ar * hi + ai * hr + bi

    fr, fi = zero, zero
    for c in range(n_chunk_ctx):
        blk = stc_s[c * bsz:(c + 1) * bsz, :]
        fr, fi = step(afr, afi, fr, fi, blk[:, 0:LANES], blk[:, LANES:2 * LANES])
    gr, gi = zero, zero
    for c in reversed(range(n_chunk_ctx)):
        blk = stc_s[c * bsz:(c + 1) * bsz, :]
        gr, gi = step(abr, abi, gr, gi, blk[:, 2 * LANES:3 * LANES], blk[:, 3 * LANES:4 * LANES])

    def fwd_body(c, carry):
        hr, hi = carry
        r0 = pl.multiple_of(c * bsz, bsz)
        fg_s[pl.ds(r0, bsz), 0:LANES] = hr
        fg_s[pl.ds(r0, bsz), LANES:2 * LANES] = hi
        return step(afr, afi, hr, hi, st_s[pl.ds(r0, bsz), 0:LANES], st_s[pl.ds(r0, bsz), LANES:2 * LANES])

    lax.fori_loop(0, n_chunk, fwd_body, (fr, fi))

    def bwd_body(k, carry):
        hr, hi = carry
        c = n_chunk - 1 - k
        r0 = pl.multiple_of(c * bsz, bsz)
        fg_s[pl.ds(r0, bsz), 2 * LANES:3 * LANES] = hr
        fg_s[pl.ds(r0, bsz), 3 * LANES:4 * LANES] = hi
        return step(abr, abi, hr, hi, st_s[pl.ds(r0, bsz), 2 * LANES:3 * LANES],
                    st_s[pl.ds(r0, bsz), 3 * LANES:4 * LANES])

    lax.fori_loop(0, n_chunk, bwd_body, (gr, gi))

    fg = fg_s[...].astype(BF16)
    nt = (((1,), (1,)), ((), ()))
    for gl in range(2):
        intra = jnp.dot(u[:, gl * w:(gl + 1) * w], wm_ref[gl], preferred_element_type=F32)
        inter = lax.dot_general(fg, qt_ref[gl], nt, preferred_element_type=F32)
        y_ref[:, gl * w:(gl + 1) * w] = intra + inter


def _s5_scan(u2, uc2, wm, wp, qt, a16, bsz):
    rows, width = u2.shape
    rows_c = uc2.shape[0]
    npair = wm.shape[0]
    w = CHUNK * S5_GROUP
    pw = 2 * w
    return pl.pallas_call(
        functools.partial(_s5_kernel, bsz=bsz, n_chunk=rows // bsz, n_chunk_ctx=rows_c // bsz),
        grid=(npair,),
        in_specs=[pl.BlockSpec((rows, pw), lambda i: (0, i)),
                  pl.BlockSpec((rows_c, pw), lambda i: (0, i)),
                  pl.BlockSpec((None, 2, w, w), lambda i: (i, 0, 0, 0)),
                  pl.BlockSpec((None, pw, 4 * LANES), lambda i: (i, 0, 0)),
                  pl.BlockSpec((None, 2, w, 4 * LANES), lambda i: (i, 0, 0, 0)),
                  pl.BlockSpec((None, 4, LANES), lambda i: (i, 0, 0))],
        out_specs=pl.BlockSpec((rows, pw), lambda i: (0, i)),
        out_shape=jax.ShapeDtypeStruct((rows, width), F32),
        scratch_shapes=[pltpu.VMEM((rows, 4 * LANES), F32), pltpu.VMEM((rows_c, 4 * LANES), F32),
                        pltpu.VMEM((rows, 4 * LANES), F32)],
        compiler_params=_cparams("parallel"),
        name="s5_scan",
    )(u2, uc2, wm, wp, qt, a16)


def _to_chunks(u, bsz, length):
    g = u.shape[-1] // S5_GROUP
    nc = length // CHUNK
    v = u.reshape(bsz, nc, CHUNK, g, S5_GROUP).transpose(1, 0, 3, 2, 4)
    return v.reshape(nc * bsz, g * CHUNK * S5_GROUP)


def _from_chunks(y2, bsz, length):
    g = y2.shape[-1] // (CHUNK * S5_GROUP)
    nc = length // CHUNK
    v = y2.reshape(nc, bsz, g, CHUNK, S5_GROUP).transpose(1, 0, 3, 2, 4)
    return v.reshape(bsz, length, g * S5_GROUP)


def _dft_kernel(c_ref, s_ref, gc_ref, gs_ref, o_ref, *, scale):
    acc = jnp.dot(c_ref[...], gc_ref[...], preferred_element_type=F32)
    acc = acc - jnp.dot(s_ref[...], gs_ref[...], preferred_element_type=F32)
    o_ref[...] = (acc * scale).astype(o_ref.dtype)


def _seq_dft(gc, gs, scale, tk):
    bsz, length, fw = gc.shape
    cm, sm = _dft_mats(length)
    cm = jnp.asarray(cm, BF16)
    sm = jnp.asarray(sm, BF16)
    return pl.pallas_call(
        functools.partial(_dft_kernel, scale=scale),
        grid=(bsz, length // tk),
        in_specs=[pl.BlockSpec((tk, length), lambda b, k: (k, 0)),
                  pl.BlockSpec((tk, length), lambda b, k: (k, 0)),
                  pl.BlockSpec((None, length, fw), lambda b, k: (b, 0, 0)),
                  pl.BlockSpec((None, length, fw), lambda b, k: (b, 0, 0))],
        out_specs=pl.BlockSpec((None, tk, fw), lambda b, k: (b, k, 0)),
        out_shape=jax.ShapeDtypeStruct((bsz, length, fw), BF16),
        compiler_params=_cparams("parallel", "arbitrary"),
        name="seq_dft",
    )(cm, sm, gc, gs)


def _outproj_kernel(y_ref, u_ref, fo_ref, x_ref, pos_ref, mods_ref, dsk_ref, wg_ref, bg_ref,
                    wo1_ref, wo2_ref, g2_ref, x1_ref, h2_ref, *, d):
    u = u_ref[...]
    y = y_ref[...] + dsk_ref[...] * u
    y = jax.nn.gelu(y)
    gate = jax.nn.sigmoid(jnp.dot(y.astype(BF16), wg_ref[...], preferred_element_type=F32) + bg_ref[...])
    s5o = (y * gate).astype(BF16)
    mix = jnp.dot(s5o, wo1_ref[...], preferred_element_type=F32)
    mix = mix + jnp.dot(fo_ref[...], wo2_ref[...], preferred_element_type=F32)
    m = mods_ref[...]
    g1 = m[:, 2 * d:3 * d]
    sh2 = m[:, 3 * d:4 * d]
    sc2 = m[:, 4 * d:5 * d]
    x1 = x_ref[...] + pos_ref[...] + g1 * mix
    x1_ref[...] = x1
    inv = lax.rsqrt(jnp.mean(x1 * x1, axis=-1, keepdims=True) + RMS_EPS)
    h2_ref[...] = ((x1 * inv) * g2_ref[...] * (1.0 + sc2) + sh2).astype(h2_ref.dtype)


def _outproj(y, u, fo, x, pos, mods3, d_skip, w_glu, b_glu, w_out, norm2_g, tl):
    bsz, length, d = x.shape
    sw = y.shape[-1]
    fw = fo.shape[-1]
    tok = lambda wd: pl.BlockSpec((None, tl, wd), lambda b, i: (b, i, 0))
    full = lambda r, c: pl.BlockSpec((r, c), lambda b, i: (0, 0))
    return pl.pallas_call(
        functools.partial(_outproj_kernel, d=d),
        grid=(bsz, length // tl),
        in_specs=[tok(sw), tok(sw), tok(fw), tok(d),
                  pl.BlockSpec((tl, d), lambda b, i: (i, 0)),
                  pl.BlockSpec((None, 1, mods3.shape[2]), lambda b, i: (b, 0, 0)),
                  full(1, sw), full(sw, sw), full(1, sw), full(sw, d), full(fw, d), full(1, d)],
        out_specs=[tok(d), tok(d)],
        out_shape=[jax.ShapeDtypeStruct((bsz, length, d), F32),
                   jax.ShapeDtypeStruct((bsz, length, d), BF16)],
        compiler_params=_cparams("parallel", "arbitrary"),
        name="outproj",
    )(y, u, fo, x, pos, mods3, d_skip.reshape(1, sw), w_glu.astype(BF16), b_glu.reshape(1, sw),
      w_out[:sw].astype(BF16), w_out[sw:].astype(BF16), norm2_g.reshape(1, d))


def _wqs_kernel(sk_ref, wq_ref, o_ref):
    nt = (((1,), (1,)), ((), ()))
    o_ref[...] = lax.dot_general(sk_ref[...], wq_ref[...], nt,
                                 preferred_element_type=F32, precision=HIGHEST)


def _fold_query_keys(w_q, sub_keys):
    d = w_q.shape[0]
    nh, _, nk, half = sub_keys.shape
    units = 2 * nh
    out = pl.pallas_call(
        _wqs_kernel,
        grid=(units,),
        in_specs=[pl.BlockSpec((None, None, nk, half), lambda u: (u % nh, u // nh, 0, 0)),
                  pl.BlockSpec((d, half), lambda u: (0, (u % nh) * 2 + u // nh))],
        out_specs=pl.BlockSpec((None, nk, d), lambda u: (u, 0, 0)),
        out_shape=jax.ShapeDtypeStruct((units, nk, d), F32),
        compiler_params=_cparams("arbitrary"),
        name="fold_query_keys",
    )(sub_keys, w_q)
    return out.transpose(1, 0, 2).reshape(nk * units, d).astype(BF16)


def _oddeven_merge_sort_pairs(n):
    pairs = []
    p = 1
    while p < n:
        k = p
        while k >= 1:
            for j in range(k % p, n - k, 2 * k):
                for i in range(min(k, n - j - k)):
                    if (i + j) // (2 * p) == (i + j + k) // (2 * p):
                        pairs.append((i + j, i + j + k))
            k //= 2
        p *= 2
    return pairs


_SORT16 = _oddeven_merge_sort_pairs(PEER_TOPK)


def _sort_desc(v):
    v = list(v)
    for i, j in _SORT16:
        hi = jnp.maximum(v[i], v[j])
        lo = jnp.minimum(v[i], v[j])
        v[i], v[j] = hi, lo
    return v


def _merge_top(a, b):
    n = len(a)
    c = [jnp.maximum(a[i], b[n - 1 - i]) for i in range(n)]
    k = n // 2
    while k >= 1:
        for i in range(n):
            if (i & k) == 0:
                hi = jnp.maximum(c[i], c[i + k])
                lo = jnp.minimum(c[i], c[i + k])
                c[i], c[i + k] = hi, lo
        k //= 2
    return c


def _top_sorted(vals):
    k = PEER_TOPK
    runs = [_sort_desc(vals[i:i + k]) for i in range(0, len(vals), k)]
    while len(runs) > 1:
        runs = [_merge_top(runs[i], runs[i + 1]) for i in range(0, len(runs), 2)]
    return runs[0]


def _cand_cells():
    k = PEER_TOPK
    return [(r0, r1) for r0 in range(k) for r1 in range(k) if (r0 + 1) * (r1 + 1) <= k]


_CELLS = _cand_cells()


def _select_block(s0, s1):
    k = PEER_TOPK
    shape = s0[0].shape
    one = jnp.ones(shape, F32)
    zero = jnp.zeros(shape, F32)
    sv0 = _top_sorted(s0)
    sv1 = _top_sorted(s1)
    cand = {c: sv0[c[0]] + sv1[c[1]] for c in _CELLS}
    cnt = {}
    for c in _CELLS:
        cnt[c] = float(sum(1 for o in _CELLS if o != c and o[0] <= c[0] and o[1] <= c[1]))
    acc = {c: None for c in _CELLS}
    for a_i, ca in enumerate(_CELLS):
        for cb in _CELLS[a_i + 1:]:
            if ca[0] <= cb[0] and ca[1] <= cb[1]:
                continue
            first = cand[ca] >= cand[cb]
            ib = jnp.where(first, one, zero)
            acc[cb] = ib if acc[cb] is None else acc[cb] + ib
            ia = one - ib
            acc[ca] = ia if acc[ca] is None else acc[ca] + ia
    e0 = [jnp.exp(sv0[r] - sv0[0]) for r in range(k)]
    e1 = [jnp.exp(sv1[r] - sv1[0]) for r in range(k)]
    nsel = [zero] * k
    zsum = zero
    for c in _CELLS:
        tot = acc[c] + cnt[c] if acc[c] is not None else jnp.full(shape, cnt[c], F32)
        sel = jnp.where(tot < float(k), one, zero)
        nsel[c[0]] = nsel[c[0]] + sel
        zsum = zsum + sel * (e0[c[0]] * e1[c[1]])
    inv_z = 1.0 / zsum
    n_l, ex0_l, r_l, ex1_l = [], [], [], []
    for e in range(len(s0)):
        n_e = zero
        r_e = jnp.full(shape, float(k), F32)
        for r in reversed(range(k)):
            n_e = jnp.where(s0[e] >= sv0[r], nsel[r], n_e)
            r_e = jnp.where(s1[e] >= sv1[r], float(r), r_e)
        n_l.append(n_e)
        r_l.append(r_e)
        ex0_l.append(jnp.exp(s0[e] - sv0[0]) * inv_z)
        ex1_l.append(jnp.exp(s1[e] - sv1[0]))
    return n_l, ex0_l, r_l, ex1_l


def _select_kernel(wqs_ref, h2_ref, n0_ref, ex0_ref, r1_ref, ex1_ref, s_s, *, tm):
    nh = PEER_HEADS
    nk = PEER_KEYS
    units = 2 * nh
    nt = (((1,), (1,)), ((), ()))
    s_s[...] = lax.dot_general(wqs_ref[...], h2_ref[...], nt, preferred_element_type=F32)

    def block(j, carry):
        c0 = pl.multiple_of(j * LANES, LANES)
        cols = pl.ds(c0, LANES)
        s0 = [s_s[e * units:e * units + nh, cols] for e in range(nk)]
        s1 = [s_s[e * units + nh:(e + 1) * units, cols] for e in range(nk)]
        n_l, ex0_l, r_l, ex1_l = _select_block(s0, s1)
        for e in range(nk):
            n0_ref[e, :, cols] = n_l[e]
            ex0_ref[e, :, cols] = ex0_l[e]
            for h in range(nh):
                r1_ref[h, pl.ds(e, 1), cols] = r_l[e][h:h + 1, :]
                ex1_ref[h, pl.ds(e, 1), cols] = ex1_l[e][h:h + 1, :]
        return carry

    lax.fori_loop(0, tm // LANES, block, 0)


def _peer_select(wqs, h2, tm):
    tokens, d = h2.shape
    nh = PEER_HEADS
    nk = PEER_KEYS
    rows = wqs.shape[0]
    o_spec = pl.BlockSpec((nk, nh, tm), lambda i: (0, 0, i))
    o_shape = jax.ShapeDtypeStruct((nk, nh, tokens), F32)
    t_spec = pl.BlockSpec((nh, nk, tm), lambda i: (0, 0, i))
    t_shape = jax.ShapeDtypeStruct((nh, nk, tokens), F32)
    return pl.pallas_call(
        functools.partial(_select_kernel, tm=tm),
        grid=(tokens // tm,),
        in_specs=[pl.BlockSpec((rows, d), lambda i: (0, 0)),
                  pl.BlockSpec((tm, d), lambda i: (i, 0))],
        out_specs=[o_spec, o_spec, t_spec, t_spec],
        out_shape=[o_shape, o_shape, t_shape, t_shape],
        scratch_shapes=[pltpu.VMEM((rows, tm), F32)],
        compiler_params=_cparams("parallel"),
        name="peer_select",
    )(wqs, h2)


def _experts_kernel(u_ref, vt_ref, h2_ref, n0_ref, ex0_ref, r1_ref, ex1_ref, x1_ref, mods_ref, fg_ref,
                    o_ref, acc_s, r1_s, ex1_s, *, d, te, tm, tq):
    nh = PEER_HEADS
    nk = PEER_KEYS
    j = pl.program_id(1)
    nt = (((1,), (1,)), ((), ()))

    @pl.when(j == 0)
    def _():
        acc_s[...] = jnp.zeros_like(acc_s)
        r1_s[...] = r1_ref[...].astype(BF16)
        ex1_s[...] = ex1_ref[...].astype(BF16)

    u = u_ref[...]
    vt = vt_ref[...]
    for q in range(tm // tq):
        cs = slice(q * tq, (q + 1) * tq)
        a = lax.dot_general(u, h2_ref[cs, :], nt, preferred_element_type=F32)
        parts = []
        for el in range(te // nk):
            n0 = n0_ref[el, :, cs].astype(BF16)
            ex0 = ex0_ref[el, :, cs].astype(BF16)
            wgt = jnp.zeros((nk, tq), BF16)
            for h in range(nh):
                ex1 = ex1_s[h, :, cs]
                keep = r1_s[h, :, cs] < n0[h:h + 1, :]
                wgt = wgt + jnp.where(keep, ex1, jnp.zeros_like(ex1)) * ex0[h:h + 1, :]
            act = jax.nn.gelu(a[el * nk:(el + 1) * nk, :])
            parts.append((act * wgt.astype(F32)).astype(BF16))
        wa = jnp.concatenate(parts, axis=0)
        acc_s[:, cs] += jnp.dot(vt, wa, preferred_element_type=F32)

    @pl.when(j == pl.num_programs(1) - 1)
    def _():
        peer = acc_s[...].T
        m = mods_ref[...]
        g2 = m[:, 5 * d:6 * d]
        xo = x1_ref[...] + g2 * peer
        inv = lax.rsqrt(jnp.mean(xo * xo, axis=-1, keepdims=True) + RMS_EPS)
        o_ref[...] = (xo * inv) * fg_ref[...]


def _peer_experts(u_bf, vt_bf, h2, n0, ex0, r1, ex1, x1, mods3, final_g, tm, te, tq):
    bsz, length, d = x1.shape
    tokens = bsz * length
    ne = u_bf.shape[0]
    nh = PEER_HEADS
    nk = PEER_KEYS
    tpb = length // tm
    once = pl.Buffered(1)
    sel0 = pl.BlockSpec((te // nk, nh, tm), lambda i, j: (j, 0, i))
    sel1 = pl.BlockSpec((nh, nk, tm), lambda i, j: (0, 0, i), pipeline_mode=once)
    return pl.pallas_call(
        functools.partial(_experts_kernel, d=d, te=te, tm=tm, tq=tq),
        grid=(tokens // tm, ne // te),
        in_specs=[pl.BlockSpec((te, d), lambda i, j: (j, 0)),
                  pl.BlockSpec((d, te), lambda i, j: (0, j)),
                  pl.BlockSpec((tm, d), lambda i, j: (i, 0), pipeline_mode=once),
                  sel0, sel0, sel1, sel1,
                  pl.BlockSpec((None, tm, d), lambda i, j: (i // tpb, i % tpb, 0), pipeline_mode=once),
                  pl.BlockSpec((None, 1, mods3.shape[2]), lambda i, j: (i // tpb, 0, 0)),
                  pl.BlockSpec((1, d), lambda i, j: (0, 0))],
        out_specs=pl.BlockSpec((None, tm, d), lambda i, j: (i // tpb, i % tpb, 0)),
        out_shape=jax.ShapeDtypeStruct((bsz, length, d), F32),
        scratch_shapes=[pltpu.VMEM((d, tm), F32),
                        pltpu.VMEM((nh, nk, tm), BF16), pltpu.VMEM((nh, nk, tm), BF16)],
        compiler_params=_cparams("parallel", "arbitrary"),
        name="peer_experts",
    )(u_bf, vt_bf, h2, n0, ex0, r1, ex1, x1, mods3, final_g.reshape(1, d))


def kernel(x, c, ctx, c_ctx, w_mod, b_mod, norm1_g, norm2_g, w_in, w_out, s5_lam_re, s5_lam_im,
           s5_log_dt, s5_b_re, s5_b_im, s5_c_re, s5_c_im, s5_d, w_glu, b_glu, w_fourier,
           peer_w_q, peer_sub_keys, peer_u, peer_v, final_g):
    bsz, length, d = x.shape
    clen = ctx.shape[1]
    sw = s5_d.shape[-1]
    fw = w_in.shape[-1] - sw
    assert w_mod.shape[0] == 1, "single layer: the context stream only feeds the S5 initial states"
    assert length % CHUNK == 0 and clen % CHUNK == 0 and bsz % 8 == 0

    tl = min(512, length)
    tlc = min(512, clen)
    tm = min(512, length)
    te = 1024

    pos = jnp.asarray(_pos_embed(length // GRID_W, d))

    mrows = -(-(bsz + 1) // 8) * 8
    c_rows = jnp.concatenate([c, c_ctx[None, :], jnp.zeros((mrows - bsz - 1, d), F32)], axis=0)
    mods = _modulation(c_rows, w_mod[0], b_mod[0])
    mods3 = mods.reshape(mrows, 1, N_MOD * d)

    wfc, wfs = _fourier_fold(w_in[0][:, sw:], w_fourier[0])
    w_all = jnp.concatenate([w_in[0][:, :sw], wfc, wfs], axis=1).astype(BF16)

    u, gc, gs = _inproj(x, pos, mods3, None, norm1_g[0], w_all, (sw, fw, fw), (F32, BF16, BF16), tl)
    (uc,) = _inproj(ctx, None, mods3, bsz, norm1_g[0], w_all[:, :sw], (sw,), (F32,), tlc)

    wm, wp, qt, a16 = _s5_prep(s5_lam_re[0], s5_lam_im[0], s5_log_dt[0], s5_b_re[0], s5_b_im[0],
                               s5_c_re[0], s5_c_im[0])
    u2 = _to_chunks(u, bsz, length).astype(BF16)
    uc2 = _to_chunks(uc, bsz, clen).astype(BF16)
    y2 = _s5_scan(u2, uc2, wm, wp, qt, a16, bsz)
    y = _from_chunks(y2, bsz, length)

    fo = _seq_dft(gc, gs, 1.0 / math.sqrt(length * (fw // FOURIER_GROUPS)), min(512, length))

    x1, h2 = _outproj(y, u, fo, x, pos, mods3, s5_d[0], w_glu[0], b_glu[0], w_out[0], norm2_g[0], tl)

    wqs = _fold_query_keys(peer_w_q[0], peer_sub_keys[0])
    h2f = h2.reshape(bsz * length, d)
    n0, ex0, r1, ex1 = _peer_select(wqs, h2f, tm)
    u_bf = peer_u[0].astype(BF16)
    vt_bf = peer_v[0].astype(BF16).T
    tme = min(1024, length)
    return _peer_experts(u_bf, vt_bf, h2f, n0, ex0, r1, ex1, x1, mods3, final_g, tme, te, min(256, tme))
```

```python
import functools
import math

import numpy as np
import jax
import jax.numpy as jnp
from jax import lax
from jax.experimental import pallas as pl
from jax.experimental.pallas import tpu as pltpu

F32 = jnp.float32
BF16 = jnp.bfloat16
HIGHEST = lax.Precision.HIGHEST

GRID_W = 64
POS_BASE = 10000.0
RMS_EPS = 1e-6
N_MOD = 6
S5_GROUP = 16
S5_STATE = 64
CHUNK = 16
FOURIER_GROUPS = 4
PEER_HEADS = 8
PEER_KEYS = 128
PEER_TOPK = 16
LANES = 128
VMEM_LIMIT = 56 * 1024 * 1024


def _cparams(*sem):
    return pltpu.CompilerParams(dimension_semantics=sem, vmem_limit_bytes=VMEM_LIMIT)


def _pos_embed(rows, dim):
    quarter = dim // 4
    omega = 1.0 / (POS_BASE ** (np.arange(quarter, dtype=np.float32) / quarter))
    r = np.repeat(np.arange(rows, dtype=np.float32), GRID_W)[:, None] * omega
    col = np.tile(np.arange(GRID_W, dtype=np.float32), rows)[:, None] * omega
    emb = np.concatenate([np.sin(r), np.cos(r), np.sin(col), np.cos(col)], axis=-1)
    return emb.astype(np.float32)


def _dft_mats(n):
    k = np.arange(n, dtype=np.int64)
    ang = 2.0 * np.pi * ((k[:, None] * k[None, :]) % n).astype(np.float64) / n
    return np.cos(ang), np.sin(ang)


def _mod_kernel(c_ref, w_ref, b_ref, o_ref):
    c = c_ref[...]
    a = c * jax.nn.sigmoid(c)
    o_ref[...] = jnp.dot(a.astype(BF16), w_ref[...].astype(BF16),
                         preferred_element_type=F32) + b_ref[...]


def _modulation(c_rows, w_mod, b_mod):
    rows, d = c_rows.shape
    n = w_mod.shape[1]
    tn = d
    return pl.pallas_call(
        _mod_kernel,
        grid=(n // tn,),
        in_specs=[pl.BlockSpec((rows, d), lambda j: (0, 0)),
                  pl.BlockSpec((d, tn), lambda j: (0, j)),
                  pl.BlockSpec((1, tn), lambda j: (0, j))],
        out_specs=pl.BlockSpec((rows, tn), lambda j: (0, j)),
        out_shape=jax.ShapeDtypeStruct((rows, n), F32),
        compiler_params=_cparams("arbitrary"),
        name="modulation",
    )(c_rows, w_mod, b_mod.reshape(1, n))


def _permute_rows(val, perm_s, n_out, stride):
    rows, width = val.shape
    nlt = width // LANES
    for k in range(nlt):
        perm_s[k] = val[:, k * LANES:(k + 1) * LANES]
    blocks = []
    for j in range(rows // n_out):
        blocks.append(jnp.concatenate(
            [perm_s[k, pl.ds(j, n_out, stride=stride), :] for k in range(nlt)], axis=1))
    return jnp.concatenate(blocks, axis=0)


def _inproj_kernel(*refs, has_pos, d, widths):
    if has_pos:
        x_ref, pos_ref, mods_ref, g_ref, w_ref = refs[:5]
        rest = refs[5:]
    else:
        x_ref, mods_ref, g_ref, w_ref = refs[:4]
        rest = refs[4:]
    outs, perm_s = rest[:-1], rest[-1]
    x = x_ref[...]
    if has_pos:
        x = x + pos_ref[...]
    inv = lax.rsqrt(jnp.mean(x * x, axis=-1, keepdims=True) + RMS_EPS)
    m = mods_ref[...]
    sh = m[:, 0:d]
    sc = m[:, d:2 * d]
    h = (x * inv) * g_ref[...] * (1.0 + sc) + sh
    z = jnp.dot(h.astype(BF16), w_ref[...], preferred_element_type=F32)
    ncl = x.shape[0] // CHUNK
    outs[0][...] = _permute_rows(z[:, 0:widths[0]], perm_s, ncl, CHUNK).reshape(outs[0].shape)
    off = 0
    for o_ref, wd in zip(outs[1:], widths):
        o_ref[...] = z[:, off:off + wd].astype(o_ref.dtype)
        off += wd


def _inproj(x, pos, mods3, mod_row, g, w, widths, dtypes, tl):
    bsz, length, d = x.shape
    has_pos = pos is not None
    nw = w.shape[1]
    if mod_row is None:
        mod_map = lambda b, i: (b, 0, 0)
    else:
        mod_map = lambda b, i: (mod_row, 0, 0)
    in_specs = [pl.BlockSpec((None, tl, d), lambda b, i: (b, i, 0))]
    args = [x]
    if has_pos:
        in_specs.append(pl.BlockSpec((tl, d), lambda b, i: (i, 0)))
        args.append(pos)
    in_specs += [pl.BlockSpec((None, 1, mods3.shape[2]), mod_map),
                 pl.BlockSpec((1, d), lambda b, i: (0, 0)),
                 pl.BlockSpec((d, nw), lambda b, i: (0, 0))]
    args += [mods3, g.reshape(1, d), w]
    ncl = tl // CHUNK
    out_specs = [pl.BlockSpec((CHUNK, ncl, widths[0]), lambda b, i: (0, i, b))]
    out_shape = [jax.ShapeDtypeStruct((CHUNK, length // CHUNK, bsz * widths[0]), F32)]
    out_specs += [pl.BlockSpec((None, tl, wd), lambda b, i: (b, i, 0)) for wd in widths[:len(dtypes)]]
    out_shape += [jax.ShapeDtypeStruct((bsz, length, wd), dt) for wd, dt in zip(widths, dtypes)]
    return pl.pallas_call(
        functools.partial(_inproj_kernel, has_pos=has_pos, d=d, widths=tuple(widths)),
        grid=(bsz, length // tl),
        in_specs=in_specs,
        out_specs=out_specs,
        out_shape=out_shape,
        scratch_shapes=[pltpu.VMEM((widths[0] // LANES, tl, LANES), F32)],
        compiler_params=_cparams("parallel", "arbitrary"),
        name="inproj_pos" if has_pos else "inproj_ctx",
    )(*args)


def _ffold_kernel(win_ref, wf_ref, cc_ref, sc_ref, oc_ref, os_ref):
    wf = wf_ref[...]
    cw = jnp.dot(cc_ref[...], wf, preferred_element_type=F32, precision=HIGHEST)
    sw = jnp.dot(sc_ref[...], wf, preferred_element_type=F32, precision=HIGHEST)
    w = win_ref[...]
    oc_ref[...] = jnp.dot(w, cw, preferred_element_type=F32, precision=HIGHEST)
    os_ref[...] = jnp.dot(w, sw, preferred_element_type=F32, precision=HIGHEST)


def _fourier_fold(w_in_f, w_fourier):
    d, fw = w_in_f.shape
    ng, gc, _ = w_fourier.shape
    cc, sc = _dft_mats(gc)
    cc = jnp.asarray(cc, F32)
    sc = jnp.asarray(sc, F32)
    return pl.pallas_call(
        _ffold_kernel,
        grid=(ng,),
        in_specs=[pl.BlockSpec((d, gc), lambda g: (0, g)),
                  pl.BlockSpec((None, gc, gc), lambda g: (g, 0, 0)),
                  pl.BlockSpec((gc, gc), lambda g: (0, 0)),
                  pl.BlockSpec((gc, gc), lambda g: (0, 0))],
        out_specs=[pl.BlockSpec((d, gc), lambda g: (0, g)),
                   pl.BlockSpec((d, gc), lambda g: (0, g))],
        out_shape=[jax.ShapeDtypeStruct((d, fw), F32), jax.ShapeDtypeStruct((d, fw), F32)],
        compiler_params=_cparams("arbitrary"),
        name="fourier_fold",
    )(w_in_f, w_fourier, cc, sc)


def _s5prep_kernel(lr_ref, li_ref, ldt_ref, btr_ref, bti_ref, cr_ref, ci_ref,
                   wm_ref, wp_ref, qt_ref, a16_ref,
                   pwr_s, pwi_s, cef_re, cef_im, ceb_re, ceb_im):
    t = CHUNK
    gp = S5_GROUP
    ns = S5_STATE
    erow = lax.broadcasted_iota(jnp.int32, (24, LANES), 0).astype(F32)
    lane = lax.broadcasted_iota(jnp.int32, (1, LANES), 1)
    gmask = [(lane < ns).astype(F32), (lane >= ns).astype(F32)]
    bbt = []
    for dr in range(2):
        lr = lr_ref[dr]
        li = li_ref[dr]
        dt = jnp.exp(ldt_ref[dr])
        mag = jnp.exp(lr * dt * erow)
        ang = li * dt * erow
        pwr_s[dr] = mag * jnp.cos(ang)
        pwi_s[dr] = mag * jnp.sin(ang)
        ar = pwr_s[dr, 1:2, :]
        ai = pwi_s[dr, 1:2, :]
        den = lr * lr + li * li
        cr = ((ar - 1.0) * lr + ai * li) / den
        ci = (ai * lr - (ar - 1.0) * li) / den
        btr = btr_ref[dr]
        bti = bti_ref[dr]
        bbt.append((cr * btr - ci * bti, cr * bti + ci * btr))
    gw = 2 * gp
    for blk in range(t + 1):
        ef = blk
        eb = t - blk
        prf = pwr_s[0, ef:ef + 1, :]
        pif = pwi_s[0, ef:ef + 1, :]
        prb = pwr_s[1, eb:eb + 1, :]
        pib = pwi_s[1, eb:eb + 1, :]
        for gl in range(2):
            m = gmask[gl]
            r0 = blk * gw + gl * gp
            cef_re[r0:r0 + gp, :] = (cr_ref[0] * prf - ci_ref[0] * pif) * m
            cef_im[r0:r0 + gp, :] = -(cr_ref[0] * pif + ci_ref[0] * prf) * m
            ceb_re[r0:r0 + gp, :] = (cr_ref[1] * prb - ci_ref[1] * pib) * m
            ceb_im[r0:r0 + gp, :] = -(cr_ref[1] * pib + ci_ref[1] * prb) * m
    w = t * gw
    lane_w = lax.broadcasted_iota(jnp.int32, (gw, w), 1)
    nt = (((1,), (1,)), ((), ()))
    btm = [[jnp.concatenate([bbt[dr][ri] * gmask[0], bbt[dr][ri] * gmask[1]], axis=0) for ri in range(2)]
           for dr in range(2)]
    btf = jnp.concatenate(btm[0], axis=1)
    btb = jnp.concatenate(btm[1], axis=1)
    cef = jnp.concatenate([cef_re[0:w, :], cef_im[0:w, :]], axis=1)
    ceb = jnp.concatenate([ceb_re[gw:gw + w, :], ceb_im[gw:gw + w, :]], axis=1)
    kf = lax.dot_general(btf, cef, nt, preferred_element_type=F32, precision=HIGHEST)
    kb = lax.dot_general(btb, ceb, nt, preferred_element_type=F32, precision=HIGHEST)
    for s in range(t):
        fwd = kf if s == 0 else pltpu.roll(kf, gw * s, 1)
        fwd = jnp.where(lane_w >= gw * s, fwd, 0.0)
        sh = (gw * (s + 1)) % w
        bwd = kb if sh == 0 else pltpu.roll(kb, sh, 1)
        bwd = jnp.where(lane_w < gw * (s + 1), bwd, 0.0)
        wm_ref[s * gw:(s + 1) * gw, :] = (fwd + bwd).astype(wm_ref.dtype)
        ef = t - 1 - s
        pr = pwr_s[0, ef:ef + 1, :]
        pi = pwi_s[0, ef:ef + 1, :]
        r0 = s * gw
        wp_ref[r0:r0 + gw, 0:LANES] = (pr * btm[0][0] - pi * btm[0][1]).astype(wp_ref.dtype)
        wp_ref[r0:r0 + gw, LANES:2 * LANES] = (pr * btm[0][1] + pi * btm[0][0]).astype(wp_ref.dtype)
        pr = pwr_s[1, s:s + 1, :]
        pi = pwi_s[1, s:s + 1, :]
        wp_ref[r0:r0 + gw, 2 * LANES:3 * LANES] = (pr * btm[1][0] - pi * btm[1][1]).astype(wp_ref.dtype)
        wp_ref[r0:r0 + gw, 3 * LANES:4 * LANES] = (pr * btm[1][1] + pi * btm[1][0]).astype(wp_ref.dtype)
    qt_ref[:, 0:LANES] = cef_re[gw:gw + w, :].astype(qt_ref.dtype)
    qt_ref[:, LANES:2 * LANES] = cef_im[gw:gw + w, :].astype(qt_ref.dtype)
    qt_ref[:, 2 * LANES:3 * LANES] = ceb_re[0:w, :].astype(qt_ref.dtype)
    qt_ref[:, 3 * LANES:4 * LANES] = ceb_im[0:w, :].astype(qt_ref.dtype)
    a16_ref[0:1, :] = pwr_s[0, t:t + 1, :]
    a16_ref[1:2, :] = pwi_s[0, t:t + 1, :]
    a16_ref[2:3, :] = pwr_s[1, t:t + 1, :]
    a16_ref[3:4, :] = pwi_s[1, t:t + 1, :]


def _s5_prep(lam_re, lam_im, log_dt, b_re, b_im, c_re, c_im):
    _, g, n = lam_re.shape
    p = b_re.shape[-1]
    npair = g // 2
    lanes = 2 * n
    lr = lam_re.reshape(2, npair, 1, lanes)
    li = lam_im.reshape(2, npair, 1, lanes)
    ldt = jnp.repeat(log_dt, n, axis=-1).reshape(2, npair, 1, lanes)

    def pair_rows(a):
        r = a.shape[2]
        return a.reshape(2, npair, 2, r, n).transpose(0, 1, 3, 2, 4).reshape(2, npair, r, lanes)

    btr = pair_rows(jnp.swapaxes(b_re, -1, -2))
    bti = pair_rows(jnp.swapaxes(b_im, -1, -2))
    cr = pair_rows(c_re)
    ci = pair_rows(c_im)
    w = CHUNK * 2 * p
    vec = pl.BlockSpec((2, None, 1, lanes), lambda i: (0, i, 0, 0))
    mat = pl.BlockSpec((2, None, p, lanes), lambda i: (0, i, 0, 0))
    wsq = pl.BlockSpec((None, w, w), lambda i: (i, 0, 0))
    ce_rows = w + 2 * p
    return pl.pallas_call(
        _s5prep_kernel,
        grid=(npair,),
        in_specs=[vec, vec, vec, mat, mat, mat, mat],
        out_specs=[wsq, wsq, wsq, pl.BlockSpec((None, 4, lanes), lambda i: (i, 0, 0))],
        out_shape=[jax.ShapeDtypeStruct((npair, w, w), BF16),
                   jax.ShapeDtypeStruct((npair, w, w), BF16),
                   jax.ShapeDtypeStruct((npair, w, w), BF16),
                   jax.ShapeDtypeStruct((npair, 4, lanes), F32)],
        scratch_shapes=[pltpu.VMEM((2, 24, lanes), F32), pltpu.VMEM((2, 24, lanes), F32),
                        pltpu.VMEM((ce_rows, lanes), F32), pltpu.VMEM((ce_rows, lanes), F32),
                        pltpu.VMEM((ce_rows, lanes), F32), pltpu.VMEM((ce_rows, lanes), F32)],
        compiler_params=_cparams("arbitrary"),
        name="s5_prep",
    )(lr, li, ldt, btr, bti, cr, ci)


def _s5_gather(ref, pr):
    gw = 2 * S5_GROUP
    ppt = LANES // gw
    seg = lax.broadcasted_iota(jnp.int32, (1, LANES), 1) // gw
    tiles = []
    for m in range(CHUNK // ppt):
        tile = None
        for i in range(ppt):
            x = ref[m * ppt + i]
            sh = (gw * (i - pr)) % LANES
            if sh:
                x = pltpu.roll(x, sh, 1)
            tile = x if tile is None else jnp.where(seg == i, x, tile)
        tiles.append(tile)
    return jnp.concatenate(tiles, axis=1).astype(BF16)


def _s5_kernel(u_ref, uc_ref, wm_ref, wp_ref, qt_ref, a16_ref, y_ref, st_s, stc_s, fg_s,
               *, bsz, n_chunk, n_chunk_ctx):
    gw = 2 * S5_GROUP
    ppt = LANES // gw
    for pr in range(ppt):
        _s5_pair(u_ref, uc_ref, wm_ref, wp_ref, qt_ref, a16_ref, y_ref, st_s, stc_s, fg_s, pr,
                 bsz=bsz, n_chunk=n_chunk, n_chunk_ctx=n_chunk_ctx)


def _s5_pair(u_ref, uc_ref, wm_ref, wp_ref, qt_ref, a16_ref, y_ref, st_s, stc_s, fg_s, pr,
             *, bsz, n_chunk, n_chunk_ctx):
    gw = 2 * S5_GROUP
    ppt = LANES // gw
    seg = lax.broadcasted_iota(jnp.int32, (1, LANES), 1) // gw
    u = _s5_gather(u_ref, pr)
    wp = wp_ref[pr]
    stc_s[...] = jnp.dot(_s5_gather(uc_ref, pr), wp, preferred_element_type=F32)
    st_s[...] = jnp.dot(u, wp, preferred_element_type=F32)
    a = a16_ref[pr]
    afr = jnp.broadcast_to(a[0:1, :], (bsz, LANES))
    afi = jnp.broadcast_to(a[1:2, :], (bsz, LANES))
    abr = jnp.broadcast_to(a[2:3, :], (bsz, LANES))
    abi = jnp.broadcast_to(a[3:4, :], (bsz, LANES))
    zero = jnp.zeros((bsz, LANES), F32)

    def step(ar, ai, hr, hi, br, bi):
        return ar * hr - ai * hi + br, ar * hi + ai * hr + bi

    fr, fi = zero, zero
    for c in range(n_chunk_ctx):
        blk = stc_s[c * bsz:(c + 1) * bsz, :]
        fr, fi = step(afr, afi, fr, fi, blk[:, 0:LANES], blk[:, LANES:2 * LANES])
    gr, gi = zero, zero
    for c in reversed(range(n_chunk_ctx)):
        blk = stc_s[c * bsz:(c + 1) * bsz, :]
        gr, gi = step(abr, abi, gr, gi, blk[:, 2 * LANES:3 * LANES], blk[:, 3 * LANES:4 * LANES])

    def fwd_body(c, carry):
        hr, hi = carry
        r0 = pl.multiple_of(c * bsz, bsz)
        fg_s[pl.ds(r0, bsz), 0:LANES] = hr
        fg_s[pl.ds(r0, bsz), LANES:2 * LANES] = hi
        return step(afr, afi, hr, hi, st_s[pl.ds(r0, bsz), 0:LANES], st_s[pl.ds(r0, bsz), LANES:2 * LANES])

    lax.fori_loop(0, n_chunk, fwd_body, (fr, fi))

    def bwd_body(k, carry):
        hr, hi = carry
        c = n_chunk - 1 - k
        r0 = pl.multiple_of(c * bsz, bsz)
        fg_s[pl.ds(r0, bsz), 2 * LANES:3 * LANES] = hr
        fg_s[pl.ds(r0, bsz), 3 * LANES:4 * LANES] = hi
        return step(abr, abi, hr, hi, st_s[pl.ds(r0, bsz), 2 * LANES:3 * LANES],
                    st_s[pl.ds(r0, bsz), 3 * LANES:4 * LANES])

    lax.fori_loop(0, n_chunk, bwd_body, (gr, gi))

    fg = fg_s[...].astype(BF16)
    nt = (((1,), (1,)), ((), ()))
    y = jnp.dot(u, wm_ref[pr], preferred_element_type=F32)
    y = y + lax.dot_general(fg, qt_ref[pr], nt, preferred_element_type=F32)
    for t in range(CHUNK):
        m, i = divmod(t, ppt)
        piece = y[:, m * LANES:(m + 1) * LANES]
        sh = (gw * (pr - i)) % LANES
        if sh:
            piece = pltpu.roll(piece, sh, 1)
        y_ref[t] = piece if pr == 0 else jnp.where(seg == pr, piece, y_ref[t])


def _s5_scan(u3, uc3, wm, wp, qt, a16, bsz):
    _, rows, width = u3.shape
    rows_c = uc3.shape[1]
    w = wm.shape[-1]
    ppt = LANES // (2 * S5_GROUP)
    once = pl.Buffered(1)
    wspec = pl.BlockSpec((ppt, w, w), lambda i: (i, 0, 0))
    return pl.pallas_call(
        functools.partial(_s5_kernel, bsz=bsz, n_chunk=rows // bsz, n_chunk_ctx=rows_c // bsz),
        grid=(width // LANES,),
        in_specs=[pl.BlockSpec((CHUNK, rows, LANES), lambda i: (0, 0, i), pipeline_mode=once),
                  pl.BlockSpec((CHUNK, rows_c, LANES), lambda i: (0, 0, i)),
                  wspec, wspec, wspec,
                  pl.BlockSpec((ppt, 4, LANES), lambda i: (i, 0, 0))],
        out_specs=pl.BlockSpec((CHUNK, rows, LANES), lambda i: (0, 0, i)),
        out_shape=jax.ShapeDtypeStruct((CHUNK, rows, width), F32),
        scratch_shapes=[pltpu.VMEM((rows, 4 * LANES), F32), pltpu.VMEM((rows_c, 4 * LANES), F32),
                        pltpu.VMEM((rows, 4 * LANES), F32)],
        compiler_params=_cparams("parallel"),
        name="s5_scan",
    )(u3, uc3, wm, wp, qt, a16)


def _dft_kernel(c_ref, s_ref, gc_ref, gs_ref, o_ref, *, scale):
    acc = jnp.dot(c_ref[...], gc_ref[...], preferred_element_type=F32)
    acc = acc - jnp.dot(s_ref[...], gs_ref[...], preferred_element_type=F32)
    o_ref[...] = (acc * scale).astype(o_ref.dtype)


def _seq_dft(gc, gs, scale, tk):
    bsz, length, fw = gc.shape
    cm, sm = _dft_mats(length)
    cm = jnp.asarray(cm, BF16)
    sm = jnp.asarray(sm, BF16)
    return pl.pallas_call(
        functools.partial(_dft_kernel, scale=scale),
        grid=(bsz, length // tk),
        in_specs=[pl.BlockSpec((tk, length), lambda b, k: (k, 0)),
                  pl.BlockSpec((tk, length), lambda b, k: (k, 0)),
                  pl.BlockSpec((None, length, fw), lambda b, k: (b, 0, 0)),
                  pl.BlockSpec((None, length, fw), lambda b, k: (b, 0, 0))],
        out_specs=pl.BlockSpec((None, tk, fw), lambda b, k: (b, k, 0)),
        out_shape=jax.ShapeDtypeStruct((bsz, length, fw), BF16),
        compiler_params=_cparams("parallel", "arbitrary"),
        name="seq_dft",
    )(cm, sm, gc, gs)


def _outproj_kernel(y_ref, u_ref, fo_ref, x_ref, pos_ref, mods_ref, dsk_ref, wg_ref, bg_ref,
                    wo1_ref, wo2_ref, g2_ref, x1_ref, h2_ref, perm_s, *, d):
    tl = x_ref.shape[0]
    ncl = tl // CHUNK
    y = _permute_rows(y_ref[...].reshape(tl, y_ref.shape[-1]), perm_s, CHUNK, ncl)
    u = u_ref[...]
    y = y + dsk_ref[...] * u
    y = jax.nn.gelu(y)
    gate = jax.nn.sigmoid(jnp.dot(y.astype(BF16), wg_ref[...], preferred_element_type=F32) + bg_ref[...])
    s5o = (y * gate).astype(BF16)
    mix = jnp.dot(s5o, wo1_ref[...], preferred_element_type=F32)
    mix = mix + jnp.dot(fo_ref[...], wo2_ref[...], preferred_element_type=F32)
    m = mods_ref[...]
    g1 = m[:, 2 * d:3 * d]
    sh2 = m[:, 3 * d:4 * d]
    sc2 = m[:, 4 * d:5 * d]
    x1 = x_ref[...] + pos_ref[...] + g1 * mix
    x1_ref[...] = x1
    inv = lax.rsqrt(jnp.mean(x1 * x1, axis=-1, keepdims=True) + RMS_EPS)
    h2_ref[...] = ((x1 * inv) * g2_ref[...] * (1.0 + sc2) + sh2).astype(h2_ref.dtype)


def _outproj(y, u, fo, x, pos, mods3, d_skip, w_glu, b_glu, w_out, norm2_g, tl):
    bsz, length, d = x.shape
    sw = d_skip.shape[-1]
    fw = fo.shape[-1]
    tok = lambda wd: pl.BlockSpec((None, tl, wd), lambda b, i: (b, i, 0))
    stp = pl.BlockSpec((CHUNK, tl // CHUNK, sw), lambda b, i: (0, i, b))
    full = lambda r, c: pl.BlockSpec((r, c), lambda b, i: (0, 0))
    return pl.pallas_call(
        functools.partial(_outproj_kernel, d=d),
        grid=(bsz, length // tl),
        in_specs=[stp, tok(sw), tok(fw), tok(d),
                  pl.BlockSpec((tl, d), lambda b, i: (i, 0)),
                  pl.BlockSpec((None, 1, mods3.shape[2]), lambda b, i: (b, 0, 0)),
                  full(1, sw), full(sw, sw), full(1, sw), full(sw, d), full(fw, d), full(1, d)],
        out_specs=[tok(d), tok(d)],
        out_shape=[jax.ShapeDtypeStruct((bsz, length, d), F32),
                   jax.ShapeDtypeStruct((bsz, length, d), BF16)],
        scratch_shapes=[pltpu.VMEM((sw // LANES, tl, LANES), F32)],
        compiler_params=_cparams("parallel", "arbitrary"),
        name="outproj",
    )(y, u, fo, x, pos, mods3, d_skip.reshape(1, sw), w_glu.astype(BF16), b_glu.reshape(1, sw),
      w_out[:sw].astype(BF16), w_out[sw:].astype(BF16), norm2_g.reshape(1, d))


def _wqs_kernel(sk_ref, wq_ref, o_ref):
    nt = (((1,), (1,)), ((), ()))
    o_ref[...] = lax.dot_general(sk_ref[...], wq_ref[...], nt,
                                 preferred_element_type=F32, precision=HIGHEST)


def _fold_query_keys(w_q, sub_keys):
    d = w_q.shape[0]
    nh, _, nk, half = sub_keys.shape
    units = 2 * nh
    out = pl.pallas_call(
        _wqs_kernel,
        grid=(units,),
        in_specs=[pl.BlockSpec((None, None, nk, half), lambda u: (u % nh, u // nh, 0, 0)),
                  pl.BlockSpec((d, half), lambda u: (0, (u % nh) * 2 + u // nh))],
        out_specs=pl.BlockSpec((None, nk, d), lambda u: (u, 0, 0)),
        out_shape=jax.ShapeDtypeStruct((units, nk, d), F32),
        compiler_params=_cparams("arbitrary"),
        name="fold_query_keys",
    )(sub_keys, w_q)
    return out.transpose(1, 0, 2).reshape(nk * units, d).astype(BF16)


def _oddeven_merge_sort_pairs(n):
    pairs = []
    p = 1
    while p < n:
        k = p
        while k >= 1:
            for j in range(k % p, n - k, 2 * k):
                for i in range(min(k, n - j - k)):
                    if (i + j) // (2 * p) == (i + j + k) // (2 * p):
                        pairs.append((i + j, i + j + k))
            k //= 2
        p *= 2
    return pairs


_SORT16 = _oddeven_merge_sort_pairs(PEER_TOPK)


def _sort_desc(v):
    v = list(v)
    for i, j in _SORT16:
        hi = jnp.maximum(v[i], v[j])
        lo = jnp.minimum(v[i], v[j])
        v[i], v[j] = hi, lo
    return v


def _merge_top(a, b):
    n = len(a)
    c = [jnp.maximum(a[i], b[n - 1 - i]) for i in range(n)]
    k = n // 2
    while k >= 1:
        for i in range(n):
            if (i & k) == 0:
                hi = jnp.maximum(c[i], c[i + k])
                lo = jnp.minimum(c[i], c[i + k])
                c[i], c[i + k] = hi, lo
        k //= 2
    return c


def _top_sorted(vals):
    k = PEER_TOPK
    runs = [_sort_desc(vals[i:i + k]) for i in range(0, len(vals), k)]
    while len(runs) > 1:
        runs = [_merge_top(runs[i], runs[i + 1]) for i in range(0, len(runs), 2)]
    return runs[0]


def _cand_cells():
    k = PEER_TOPK
    return [(r0, r1) for r0 in range(k) for r1 in range(k) if (r0 + 1) * (r1 + 1) <= k]


_CELLS = _cand_cells()


def _select_block(s0, s1):
    k = PEER_TOPK
    shape = s0[0].shape
    one = jnp.ones(shape, F32)
    zero = jnp.zeros(shape, F32)
    sv0 = _top_sorted(s0)
    sv1 = _top_sorted(s1)
    cand = {c: sv0[c[0]] + sv1[c[1]] for c in _CELLS}
    cnt = {}
    for c in _CELLS:
        cnt[c] = float(sum(1 for o in _CELLS if o != c and o[0] <= c[0] and o[1] <= c[1]))
    acc = {c: None for c in _CELLS}
    for a_i, ca in enumerate(_CELLS):
        for cb in _CELLS[a_i + 1:]:
            if ca[0] <= cb[0] and ca[1] <= cb[1]:
                continue
            first = cand[ca] >= cand[cb]
            ib = jnp.where(first, one, zero)
            acc[cb] = ib if acc[cb] is None else acc[cb] + ib
            ia = one - ib
            acc[ca] = ia if acc[ca] is None else acc[ca] + ia
    e0 = [jnp.exp(sv0[r] - sv0[0]) for r in range(k)]
    e1 = [jnp.exp(sv1[r] - sv1[0]) for r in range(k)]
    nsel = [zero] * k
    zsum = zero
    for c in _CELLS:
        tot = acc[c] + cnt[c] if acc[c] is not None else jnp.full(shape, cnt[c], F32)
        sel = jnp.where(tot < float(k), one, zero)
        nsel[c[0]] = nsel[c[0]] + sel
        zsum = zsum + sel * (e0[c[0]] * e1[c[1]])
    inv_z = 1.0 / zsum
    n_l, ex0_l, r_l, ex1_l = [], [], [], []
    for e in range(len(s0)):
        n_e = zero
        r_e = jnp.full(shape, float(k), F32)
        for r in reversed(range(k)):
            n_e = jnp.where(s0[e] >= sv0[r], nsel[r], n_e)
            r_e = jnp.where(s1[e] >= sv1[r], float(r), r_e)
        n_l.append(n_e)
        r_l.append(r_e)
        ex0_l.append(jnp.exp(s0[e] - sv0[0]) * inv_z)
        ex1_l.append(jnp.exp(s1[e] - sv1[0]))
    return n_l, ex0_l, r_l, ex1_l


def _select_kernel(wqs_ref, h2_ref, n0_ref, ex0_ref, r1_ref, ex1_ref, s_s, *, tm):
    nh = PEER_HEADS
    nk = PEER_KEYS
    units = 2 * nh
    nt = (((1,), (1,)), ((), ()))
    s_s[...] = lax.dot_general(wqs_ref[...], h2_ref[...], nt, preferred_element_type=F32)

    def block(j, carry):
        c0 = pl.multiple_of(j * LANES, LANES)
        cols = pl.ds(c0, LANES)
        s0 = [s_s[e * units:e * units + nh, cols] for e in range(nk)]
        s1 = [s_s[e * units + nh:(e + 1) * units, cols] for e in range(nk)]
        n_l, ex0_l, r_l, ex1_l = _select_block(s0, s1)
        for e in range(nk):
            n0_ref[e, :, cols] = n_l[e]
            ex0_ref[e, :, cols] = ex0_l[e]
            for h in range(nh):
                r1_ref[h, pl.ds(e, 1), cols] = r_l[e][h:h + 1, :]
                ex1_ref[h, pl.ds(e, 1), cols] = ex1_l[e][h:h + 1, :]
        return carry

    lax.fori_loop(0, tm // LANES, block, 0)


def _peer_select(wqs, h2, tm):
    tokens, d = h2.shape
    nh = PEER_HEADS
    nk = PEER_KEYS
    rows = wqs.shape[0]
    o_spec = pl.BlockSpec((nk, nh, tm), lambda i: (0, 0, i))
    o_shape = jax.ShapeDtypeStruct((nk, nh, tokens), F32)
    t_spec = pl.BlockSpec((nh, nk, tm), lambda i: (0, 0, i))
    t_shape = jax.ShapeDtypeStruct((nh, nk, tokens), F32)
    return pl.pallas_call(
        functools.partial(_select_kernel, tm=tm),
        grid=(tokens // tm,),
        in_specs=[pl.BlockSpec((rows, d), lambda i: (0, 0)),
                  pl.BlockSpec((tm, d), lambda i: (i, 0))],
        out_specs=[o_spec, o_spec, t_spec, t_spec],
        out_shape=[o_shape, o_shape, t_shape, t_shape],
        scratch_shapes=[pltpu.VMEM((rows, tm), F32)],
        compiler_params=_cparams("parallel"),
        name="peer_select",
    )(wqs, h2)


def _experts_kernel(u_ref, vt_ref, h2_ref, n0_ref, ex0_ref, r1_ref, ex1_ref, x1_ref, mods_ref, fg_ref,
                    o_ref, acc_s, r1_s, ex1_s, *, d, te, tm, tq):
    nh = PEER_HEADS
    nk = PEER_KEYS
    j = pl.program_id(1)
    nt = (((1,), (1,)), ((), ()))

    @pl.when(j == 0)
    def _():
        acc_s[...] = jnp.zeros_like(acc_s)
        r1_s[...] = r1_ref[...].astype(BF16)
        ex1_s[...] = ex1_ref[...].astype(BF16)

    u = u_ref[...]
    vt = vt_ref[...]
    nq = tm // tq

    def scores(q):
        return lax.dot_general(u, h2_ref[q * tq:(q + 1) * tq, :], nt, preferred_element_type=F32)

    a_next = scores(0)
    for q in range(nq):
        cs = slice(q * tq, (q + 1) * tq)
        a = a_next
        if q + 1 < nq:
            a_next = scores(q + 1)
        parts = []
        for el in range(te // nk):
            n0 = n0_ref[el, :, cs].astype(BF16)
            ex0 = ex0_ref[el, :, cs].astype(BF16)
            wgt = jnp.zeros((nk, tq), BF16)
            for h in range(nh):
                ex1 = ex1_s[h, :, cs]
                keep = r1_s[h, :, cs] < n0[h:h + 1, :]
                wgt = wgt + jnp.where(keep, ex1, jnp.zeros_like(ex1)) * ex0[h:h + 1, :]
            act = jax.nn.gelu(a[el * nk:(el + 1) * nk, :].astype(BF16))
            parts.append(act * wgt)
        wa = jnp.concatenate(parts, axis=0)
        acc_s[:, cs] += jnp.dot(vt, wa, preferred_element_type=F32)

    @pl.when(j == pl.num_programs(1) - 1)
    def _():
        peer = acc_s[...].T
        m = mods_ref[...]
        g2 = m[:, 5 * d:6 * d]
        xo = x1_ref[...] + g2 * peer
        inv = lax.rsqrt(jnp.mean(xo * xo, axis=-1, keepdims=True) + RMS_EPS)
        o_ref[...] = (xo * inv) * fg_ref[...]


def _peer_experts(u_bf, vt_bf, h2, n0, ex0, r1, ex1, x1, mods3, final_g, tm, te, tq):
    bsz, length, d = x1.shape
    tokens = bsz * length
    ne = u_bf.shape[0]
    nh = PEER_HEADS
    nk = PEER_KEYS
    tpb = length // tm
    once = pl.Buffered(1)
    sel0 = pl.BlockSpec((te // nk, nh, tm), lambda i, j: (j, 0, i))
    sel1 = pl.BlockSpec((nh, nk, tm), lambda i, j: (0, 0, i), pipeline_mode=once)
    return pl.pallas_call(
        functools.partial(_experts_kernel, d=d, te=te, tm=tm, tq=tq),
        grid=(tokens // tm, ne // te),
        in_specs=[pl.BlockSpec((te, d), lambda i, j: (j, 0)),
                  pl.BlockSpec((d, te), lambda i, j: (0, j)),
                  pl.BlockSpec((tm, d), lambda i, j: (i, 0), pipeline_mode=once),
                  sel0, sel0, sel1, sel1,
                  pl.BlockSpec((None, tm, d), lambda i, j: (i // tpb, i % tpb, 0), pipeline_mode=once),
                  pl.BlockSpec((None, 1, mods3.shape[2]), lambda i, j: (i // tpb, 0, 0)),
                  pl.BlockSpec((1, d), lambda i, j: (0, 0))],
        out_specs=pl.BlockSpec((None, tm, d), lambda i, j: (i // tpb, i % tpb, 0)),
        out_shape=jax.ShapeDtypeStruct((bsz, length, d), F32),
        scratch_shapes=[pltpu.VMEM((d, tm), F32),
                        pltpu.VMEM((nh, nk, tm), BF16), pltpu.VMEM((nh, nk, tm), BF16)],
        compiler_params=_cparams("parallel", "arbitrary"),
        name="peer_experts",
    )(u_bf, vt_bf, h2, n0, ex0, r1, ex1, x1, mods3, final_g.reshape(1, d))


def kernel(x, c, ctx, c_ctx, w_mod, b_mod, norm1_g, norm2_g, w_in, w_out, s5_lam_re, s5_lam_im,
           s5_log_dt, s5_b_re, s5_b_im, s5_c_re, s5_c_im, s5_d, w_glu, b_glu, w_fourier,
           peer_w_q, peer_sub_keys, peer_u, peer_v, final_g):
    bsz, length, d = x.shape
    clen = ctx.shape[1]
    sw = s5_d.shape[-1]
    fw = w_in.shape[-1] - sw
    assert w_mod.shape[0] == 1, "single layer: the context stream only feeds the S5 initial states"
    assert length % CHUNK == 0 and clen % CHUNK == 0 and bsz % 8 == 0

    tl = min(512, length)
    tlc = min(512, clen)
    tm = min(512, length)
    te = 1024

    pos = jnp.asarray(_pos_embed(length // GRID_W, d))

    mrows = -(-(bsz + 1) // 8) * 8
    c_rows = jnp.concatenate([c, c_ctx[None, :], jnp.zeros((mrows - bsz - 1, d), F32)], axis=0)
    mods = _modulation(c_rows, w_mod[0], b_mod[0])
    mods3 = mods.reshape(mrows, 1, N_MOD * d)

    wfc, wfs = _fourier_fold(w_in[0][:, sw:], w_fourier[0])
    w_all = jnp.concatenate([w_in[0][:, :sw], wfc, wfs], axis=1).astype(BF16)

    u3, u, gc, gs = _inproj(x, pos, mods3, None, norm1_g[0], w_all, (sw, fw, fw), (F32, BF16, BF16), tl)
    (uc3,) = _inproj(ctx, None, mods3, bsz, norm1_g[0], w_all[:, :sw], (sw,), (), tlc)

    wm, wp, qt, a16 = _s5_prep(s5_lam_re[0], s5_lam_im[0], s5_log_dt[0], s5_b_re[0], s5_b_im[0],
                               s5_c_re[0], s5_c_im[0])
    y3 = _s5_scan(u3.reshape(CHUNK, -1, sw), uc3.reshape(CHUNK, -1, sw), wm, wp, qt, a16, bsz)
    y = y3.reshape(u3.shape)

    fo = _seq_dft(gc, gs, 1.0 / math.sqrt(length * (fw // FOURIER_GROUPS)), min(512, length))

    x1, h2 = _outproj(y, u, fo, x, pos, mods3, s5_d[0], w_glu[0], b_glu[0], w_out[0], norm2_g[0], tl)

    wqs = _fold_query_keys(peer_w_q[0], peer_sub_keys[0])
    h2f = h2.reshape(bsz * length, d)
    n0, ex0, r1, ex1 = _peer_select(wqs, h2f, tm)
    u_bf = peer_u[0].astype(BF16)
    vt_bf = peer_v[0].astype(BF16).T
    tme = min(1024, length)
    return _peer_experts(u_bf, vt_bf, h2f, n0, ex0, r1, ex1, x1, mods3, final_g, tme, te, min(256, tme))
```

```python
import functools
import math

import numpy as np
import jax
import jax.numpy as jnp
from jax import lax
from jax.experimental import pallas as pl
from jax.experimental.pallas import tpu as pltpu

F32 = jnp.float32
BF16 = jnp.bfloat16
HIGHEST = lax.Precision.HIGHEST

GRID_W = 64
POS_BASE = 10000.0
RMS_EPS = 1e-6
N_MOD = 6
S5_GROUP = 16
S5_STATE = 64
CHUNK = 16
FOURIER_GROUPS = 4
PEER_HEADS = 8
PEER_KEYS = 128
PEER_TOPK = 16
LANES = 128
VMEM_LIMIT = 56 * 1024 * 1024


def _cparams(*sem):
    return pltpu.CompilerParams(dimension_semantics=sem, vmem_limit_bytes=VMEM_LIMIT)


def _pos_embed(rows, dim):
    quarter = dim // 4
    omega = 1.0 / (POS_BASE ** (np.arange(quarter, dtype=np.float32) / quarter))
    r = np.repeat(np.arange(rows, dtype=np.float32), GRID_W)[:, None] * omega
    col = np.tile(np.arange(GRID_W, dtype=np.float32), rows)[:, None] * omega
    emb = np.concatenate([np.sin(r), np.cos(r), np.sin(col), np.cos(col)], axis=-1)
    return emb.astype(np.float32)


def _dft_mats(n):
    k = np.arange(n, dtype=np.int64)
    ang = 2.0 * np.pi * ((k[:, None] * k[None, :]) % n).astype(np.float64) / n
    return np.cos(ang), np.sin(ang)


def _mod_kernel(c_ref, w_ref, b_ref, o_ref):
    c = c_ref[...]
    a = c * jax.nn.sigmoid(c)
    o_ref[...] = jnp.dot(a.astype(BF16), w_ref[...].astype(BF16),
                         preferred_element_type=F32) + b_ref[...]


def _modulation(c_rows, w_mod, b_mod):
    rows, d = c_rows.shape
    n = w_mod.shape[1]
    tn = d
    return pl.pallas_call(
        _mod_kernel,
        grid=(n // tn,),
        in_specs=[pl.BlockSpec((rows, d), lambda j: (0, 0)),
                  pl.BlockSpec((d, tn), lambda j: (0, j)),
                  pl.BlockSpec((1, tn), lambda j: (0, j))],
        out_specs=pl.BlockSpec((rows, tn), lambda j: (0, j)),
        out_shape=jax.ShapeDtypeStruct((rows, n), F32),
        compiler_params=_cparams("arbitrary"),
        name="modulation",
    )(c_rows, w_mod, b_mod.reshape(1, n))


def _permute_rows(val, perm_s, n_out, stride):
    rows, width = val.shape
    nlt = width // LANES
    for k in range(nlt):
        perm_s[k] = val[:, k * LANES:(k + 1) * LANES]
    blocks = []
    for j in range(rows // n_out):
        blocks.append(jnp.concatenate(
            [perm_s[k, pl.ds(j, n_out, stride=stride), :] for k in range(nlt)], axis=1))
    return jnp.concatenate(blocks, axis=0)


def _inproj_kernel(*refs, has_pos, d, widths):
    if has_pos:
        x_ref, pos_ref, mods_ref, g_ref, w_ref = refs[:5]
        rest = refs[5:]
    else:
        x_ref, mods_ref, g_ref, w_ref = refs[:4]
        rest = refs[4:]
    outs, perm_s = rest[:-1], rest[-1]
    x = x_ref[...]
    if has_pos:
        x = x + pos_ref[...]
    inv = lax.rsqrt(jnp.mean(x * x, axis=-1, keepdims=True) + RMS_EPS)
    m = mods_ref[...]
    sh = m[:, 0:d]
    sc = m[:, d:2 * d]
    h = (x * inv) * g_ref[...] * (1.0 + sc) + sh
    z = jnp.dot(h.astype(BF16), w_ref[...], preferred_element_type=F32)
    ncl = x.shape[0] // CHUNK
    outs[0][...] = _permute_rows(z[:, 0:widths[0]], perm_s, ncl, CHUNK).reshape(outs[0].shape)
    off = 0
    for o_ref, wd in zip(outs[1:], widths):
        o_ref[...] = z[:, off:off + wd].astype(o_ref.dtype)
        off += wd


def _inproj(x, pos, mods3, mod_row, g, w, widths, dtypes, tl):
    bsz, length, d = x.shape
    has_pos = pos is not None
    nw = w.shape[1]
    if mod_row is None:
        mod_map = lambda b, i: (b, 0, 0)
    else:
        mod_map = lambda b, i: (mod_row, 0, 0)
    in_specs = [pl.BlockSpec((None, tl, d), lambda b, i: (b, i, 0))]
    args = [x]
    if has_pos:
        in_specs.append(pl.BlockSpec((tl, d), lambda b, i: (i, 0)))
        args.append(pos)
    in_specs += [pl.BlockSpec((None, 1, mods3.shape[2]), mod_map),
                 pl.BlockSpec((1, d), lambda b, i: (0, 0)),
                 pl.BlockSpec((d, nw), lambda b, i: (0, 0))]
    args += [mods3, g.reshape(1, d), w]
    ncl = tl // CHUNK
    out_specs = [pl.BlockSpec((CHUNK, ncl, widths[0]), lambda b, i: (0, i, b))]
    out_shape = [jax.ShapeDtypeStruct((CHUNK, length // CHUNK, bsz * widths[0]), F32)]
    out_specs += [pl.BlockSpec((None, tl, wd), lambda b, i: (b, i, 0)) for wd in widths[:len(dtypes)]]
    out_shape += [jax.ShapeDtypeStruct((bsz, length, wd), dt) for wd, dt in zip(widths, dtypes)]
    return pl.pallas_call(
        functools.partial(_inproj_kernel, has_pos=has_pos, d=d, widths=tuple(widths)),
        grid=(bsz, length // tl),
        in_specs=in_specs,
        out_specs=out_specs,
        out_shape=out_shape,
        scratch_shapes=[pltpu.VMEM((widths[0] // LANES, tl, LANES), F32)],
        compiler_params=_cparams("parallel", "arbitrary"),
        name="inproj_pos" if has_pos else "inproj_ctx",
    )(*args)


def _ffold_kernel(win_ref, wf_ref, cc_ref, sc_ref, oc_ref, os_ref):
    wf = wf_ref[...]
    cw = jnp.dot(cc_ref[...], wf, preferred_element_type=F32, precision=HIGHEST)
    sw = jnp.dot(sc_ref[...], wf, preferred_element_type=F32, precision=HIGHEST)
    w = win_ref[...]
    oc_ref[...] = jnp.dot(w, cw, preferred_element_type=F32, precision=HIGHEST)
    os_ref[...] = jnp.dot(w, sw, preferred_element_type=F32, precision=HIGHEST)


def _fourier_fold(w_in_f, w_fourier):
    d, fw = w_in_f.shape
    ng, gc, _ = w_fourier.shape
    cc, sc = _dft_mats(gc)
    cc = jnp.asarray(cc, F32)
    sc = jnp.asarray(sc, F32)
    return pl.pallas_call(
        _ffold_kernel,
        grid=(ng,),
        in_specs=[pl.BlockSpec((d, gc), lambda g: (0, g)),
                  pl.BlockSpec((None, gc, gc), lambda g: (g, 0, 0)),
                  pl.BlockSpec((gc, gc), lambda g: (0, 0)),
                  pl.BlockSpec((gc, gc), lambda g: (0, 0))],
        out_specs=[pl.BlockSpec((d, gc), lambda g: (0, g)),
                   pl.BlockSpec((d, gc), lambda g: (0, g))],
        out_shape=[jax.ShapeDtypeStruct((d, fw), F32), jax.ShapeDtypeStruct((d, fw), F32)],
        compiler_params=_cparams("arbitrary"),
        name="fourier_fold",
    )(w_in_f, w_fourier, cc, sc)


def _s5prep_kernel(lr_ref, li_ref, ldt_ref, btr_ref, bti_ref, cr_ref, ci_ref,
                   wm_ref, wp_ref, qt_ref, a16_ref,
                   pwr_s, pwi_s, cef_re, cef_im, ceb_re, ceb_im):
    t = CHUNK
    gp = S5_GROUP
    ns = S5_STATE
    erow = lax.broadcasted_iota(jnp.int32, (24, LANES), 0).astype(F32)
    lane = lax.broadcasted_iota(jnp.int32, (1, LANES), 1)
    gmask = [(lane < ns).astype(F32), (lane >= ns).astype(F32)]
    bbt = []
    for dr in range(2):
        lr = lr_ref[dr]
        li = li_ref[dr]
        dt = jnp.exp(ldt_ref[dr])
        mag = jnp.exp(lr * dt * erow)
        ang = li * dt * erow
        pwr_s[dr] = mag * jnp.cos(ang)
        pwi_s[dr] = mag * jnp.sin(ang)
        ar = pwr_s[dr, 1:2, :]
        ai = pwi_s[dr, 1:2, :]
        den = lr * lr + li * li
        cr = ((ar - 1.0) * lr + ai * li) / den
        ci = (ai * lr - (ar - 1.0) * li) / den
        btr = btr_ref[dr]
        bti = bti_ref[dr]
        bbt.append((cr * btr - ci * bti, cr * bti + ci * btr))
    gw = 2 * gp
    for blk in range(t + 1):
        ef = blk
        eb = t - blk
        prf = pwr_s[0, ef:ef + 1, :]
        pif = pwi_s[0, ef:ef + 1, :]
        prb = pwr_s[1, eb:eb + 1, :]
        pib = pwi_s[1, eb:eb + 1, :]
        for gl in range(2):
            m = gmask[gl]
            r0 = blk * gw + gl * gp
            cef_re[r0:r0 + gp, :] = (cr_ref[0] * prf - ci_ref[0] * pif) * m
            cef_im[r0:r0 + gp, :] = -(cr_ref[0] * pif + ci_ref[0] * prf) * m
            ceb_re[r0:r0 + gp, :] = (cr_ref[1] * prb - ci_ref[1] * pib) * m
            ceb_im[r0:r0 + gp, :] = -(cr_ref[1] * pib + ci_ref[1] * prb) * m
    w = t * gw
    lane_w = lax.broadcasted_iota(jnp.int32, (gw, w), 1)
    nt = (((1,), (1,)), ((), ()))
    btm = [[jnp.concatenate([bbt[dr][ri] * gmask[0], bbt[dr][ri] * gmask[1]], axis=0) for ri in range(2)]
           for dr in range(2)]
    btf = jnp.concatenate(btm[0], axis=1)
    btb = jnp.concatenate(btm[1], axis=1)
    cef = jnp.concatenate([cef_re[0:w, :], cef_im[0:w, :]], axis=1)
    ceb = jnp.concatenate([ceb_re[gw:gw + w, :], ceb_im[gw:gw + w, :]], axis=1)
    kf = lax.dot_general(btf, cef, nt, preferred_element_type=F32, precision=HIGHEST)
    kb = lax.dot_general(btb, ceb, nt, preferred_element_type=F32, precision=HIGHEST)
    for s in range(t):
        fwd = kf if s == 0 else pltpu.roll(kf, gw * s, 1)
        fwd = jnp.where(lane_w >= gw * s, fwd, 0.0)
        sh = (gw * (s + 1)) % w
        bwd = kb if sh == 0 else pltpu.roll(kb, sh, 1)
        bwd = jnp.where(lane_w < gw * (s + 1), bwd, 0.0)
        wm_ref[s * gw:(s + 1) * gw, :] = (fwd + bwd).astype(wm_ref.dtype)
        ef = t - 1 - s
        pr = pwr_s[0, ef:ef + 1, :]
        pi = pwi_s[0, ef:ef + 1, :]
        r0 = s * gw
        wp_ref[r0:r0 + gw, 0:LANES] = (pr * btm[0][0] - pi * btm[0][1]).astype(wp_ref.dtype)
        wp_ref[r0:r0 + gw, LANES:2 * LANES] = (pr * btm[0][1] + pi * btm[0][0]).astype(wp_ref.dtype)
        pr = pwr_s[1, s:s + 1, :]
        pi = pwi_s[1, s:s + 1, :]
        wp_ref[r0:r0 + gw, 2 * LANES:3 * LANES] = (pr * btm[1][0] - pi * btm[1][1]).astype(wp_ref.dtype)
        wp_ref[r0:r0 + gw, 3 * LANES:4 * LANES] = (pr * btm[1][1] + pi * btm[1][0]).astype(wp_ref.dtype)
    qt_ref[:, 0:LANES] = cef_re[gw:gw + w, :].astype(qt_ref.dtype)
    qt_ref[:, LANES:2 * LANES] = cef_im[gw:gw + w, :].astype(qt_ref.dtype)
    qt_ref[:, 2 * LANES:3 * LANES] = ceb_re[0:w, :].astype(qt_ref.dtype)
    qt_ref[:, 3 * LANES:4 * LANES] = ceb_im[0:w, :].astype(qt_ref.dtype)
    a16_ref[0:1, :] = pwr_s[0, t:t + 1, :]
    a16_ref[1:2, :] = pwi_s[0, t:t + 1, :]
    a16_ref[2:3, :] = pwr_s[1, t:t + 1, :]
    a16_ref[3:4, :] = pwi_s[1, t:t + 1, :]


def _s5_prep(lam_re, lam_im, log_dt, b_re, b_im, c_re, c_im):
    _, g, n = lam_re.shape
    p = b_re.shape[-1]
    npair = g // 2
    lanes = 2 * n
    lr = lam_re.reshape(2, npair, 1, lanes)
    li = lam_im.reshape(2, npair, 1, lanes)
    ldt = jnp.repeat(log_dt, n, axis=-1).reshape(2, npair, 1, lanes)

    def pair_rows(a):
        r = a.shape[2]
        return a.reshape(2, npair, 2, r, n).transpose(0, 1, 3, 2, 4).reshape(2, npair, r, lanes)

    btr = pair_rows(jnp.swapaxes(b_re, -1, -2))
    bti = pair_rows(jnp.swapaxes(b_im, -1, -2))
    cr = pair_rows(c_re)
    ci = pair_rows(c_im)
    w = CHUNK * 2 * p
    vec = pl.BlockSpec((2, None, 1, lanes), lambda i: (0, i, 0, 0))
    mat = pl.BlockSpec((2, None, p, lanes), lambda i: (0, i, 0, 0))
    wsq = pl.BlockSpec((None, w, w), lambda i: (i, 0, 0))
    ce_rows = w + 2 * p
    return pl.pallas_call(
        _s5prep_kernel,
        grid=(npair,),
        in_specs=[vec, vec, vec, mat, mat, mat, mat],
        out_specs=[wsq, wsq, wsq, pl.BlockSpec((None, 4, lanes), lambda i: (i, 0, 0))],
        out_shape=[jax.ShapeDtypeStruct((npair, w, w), BF16),
                   jax.ShapeDtypeStruct((npair, w, w), BF16),
                   jax.ShapeDtypeStruct((npair, w, w), BF16),
                   jax.ShapeDtypeStruct((npair, 4, lanes), F32)],
        scratch_shapes=[pltpu.VMEM((2, 24, lanes), F32), pltpu.VMEM((2, 24, lanes), F32),
                        pltpu.VMEM((ce_rows, lanes), F32), pltpu.VMEM((ce_rows, lanes), F32),
                        pltpu.VMEM((ce_rows, lanes), F32), pltpu.VMEM((ce_rows, lanes), F32)],
        compiler_params=_cparams("arbitrary"),
        name="s5_prep",
    )(lr, li, ldt, btr, bti, cr, ci)


def _s5_gather(ref, pr):
    gw = 2 * S5_GROUP
    ppt = LANES // gw
    seg = lax.broadcasted_iota(jnp.int32, (1, LANES), 1) // gw
    tiles = []
    for m in range(CHUNK // ppt):
        tile = None
        for i in range(ppt):
            x = ref[m * ppt + i]
            sh = (gw * (i - pr)) % LANES
            if sh:
                x = pltpu.roll(x, sh, 1)
            tile = x if tile is None else jnp.where(seg == i, x, tile)
        tiles.append(tile)
    return jnp.concatenate(tiles, axis=1).astype(BF16)


def _s5_kernel(u_ref, uc_ref, wm_ref, wp_ref, qt_ref, a16_ref, y_ref, st_s, stc_s, fg_s,
               *, bsz, n_chunk, n_chunk_ctx):
    gw = 2 * S5_GROUP
    ppt = LANES // gw
    for pr in range(ppt):
        _s5_pair(u_ref, uc_ref, wm_ref, wp_ref, qt_ref, a16_ref, y_ref, st_s, stc_s, fg_s, pr,
                 bsz=bsz, n_chunk=n_chunk, n_chunk_ctx=n_chunk_ctx)


def _s5_pair(u_ref, uc_ref, wm_ref, wp_ref, qt_ref, a16_ref, y_ref, st_s, stc_s, fg_s, pr,
             *, bsz, n_chunk, n_chunk_ctx):
    gw = 2 * S5_GROUP
    ppt = LANES // gw
    seg = lax.broadcasted_iota(jnp.int32, (1, LANES), 1) // gw
    u = _s5_gather(u_ref, pr)
    wp = wp_ref[pr]
    stc_s[...] = jnp.dot(_s5_gather(uc_ref, pr), wp, preferred_element_type=F32)
    st_s[...] = jnp.dot(u, wp, preferred_element_type=F32)
    a = a16_ref[pr]
    afr = jnp.broadcast_to(a[0:1, :], (bsz, LANES))
    afi = jnp.broadcast_to(a[1:2, :], (bsz, LANES))
    abr = jnp.broadcast_to(a[2:3, :], (bsz, LANES))
    abi = jnp.broadcast_to(a[3:4, :], (bsz, LANES))
    zero = jnp.zeros((bsz, LANES), F32)

    def step(ar, ai, hr, hi, br, bi):
        return ar * hr - ai * hi + br, ar * hi + ai * hr + bi

    fr, fi = zero, zero
    for c in range(n_chunk_ctx):
        blk = stc_s[c * bsz:(c + 1) * bsz, :]
        fr, fi = step(afr, afi, fr, fi, blk[:, 0:LANES], blk[:, LANES:2 * LANES])
    gr, gi = zero, zero
    for c in reversed(range(n_chunk_ctx)):
        blk = stc_s[c * bsz:(c + 1) * bsz, :]
        gr, gi = step(abr, abi, gr, gi, blk[:, 2 * LANES:3 * LANES], blk[:, 3 * LANES:4 * LANES])

    def fwd_body(c, carry):
        hr, hi = carry
        r0 = pl.multiple_of(c * bsz, bsz)
        fg_s[pl.ds(r0, bsz), 0:LANES] = hr
        fg_s[pl.ds(r0, bsz), LANES:2 * LANES] = hi
        return step(afr, afi, hr, hi, st_s[pl.ds(r0, bsz), 0:LANES], st_s[pl.ds(r0, bsz), LANES:2 * LANES])

    lax.fori_loop(0, n_chunk, fwd_body, (fr, fi))

    def bwd_body(k, carry):
        hr, hi = carry
        c = n_chunk - 1 - k
        r0 = pl.multiple_of(c * bsz, bsz)
        fg_s[pl.ds(r0, bsz), 2 * LANES:3 * LANES] = hr
        fg_s[pl.ds(r0, bsz), 3 * LANES:4 * LANES] = hi
        return step(abr, abi, hr, hi, st_s[pl.ds(r0, bsz), 2 * LANES:3 * LANES],
                    st_s[pl.ds(r0, bsz), 3 * LANES:4 * LANES])

    lax.fori_loop(0, n_chunk, bwd_body, (gr, gi))

    fg = fg_s[...].astype(BF16)
    nt = (((1,), (1,)), ((), ()))
    y = jnp.dot(u, wm_ref[pr], preferred_element_type=F32)
    y = y + lax.dot_general(fg, qt_ref[pr], nt, preferred_element_type=F32)
    for t in range(CHUNK):
        m, i = divmod(t, ppt)
        piece = y[:, m * LANES:(m + 1) * LANES]
        sh = (gw * (pr - i)) % LANES
        if sh:
            piece = pltpu.roll(piece, sh, 1)
        y_ref[t] = piece if pr == 0 else jnp.where(seg == pr, piece, y_ref[t])


def _s5_scan(u3, uc3, wm, wp, qt, a16, bsz):
    _, rows, width = u3.shape
    rows_c = uc3.shape[1]
    w = wm.shape[-1]
    ppt = LANES // (2 * S5_GROUP)
    once = pl.Buffered(1)
    wspec = pl.BlockSpec((ppt, w, w), lambda i: (i, 0, 0))
    return pl.pallas_call(
        functools.partial(_s5_kernel, bsz=bsz, n_chunk=rows // bsz, n_chunk_ctx=rows_c // bsz),
        grid=(width // LANES,),
        in_specs=[pl.BlockSpec((CHUNK, rows, LANES), lambda i: (0, 0, i), pipeline_mode=once),
                  pl.BlockSpec((CHUNK, rows_c, LANES), lambda i: (0, 0, i)),
                  wspec, wspec, wspec,
                  pl.BlockSpec((ppt, 4, LANES), lambda i: (i, 0, 0))],
        out_specs=pl.BlockSpec((CHUNK, rows, LANES), lambda i: (0, 0, i)),
        out_shape=jax.ShapeDtypeStruct((CHUNK, rows, width), F32),
        scratch_shapes=[pltpu.VMEM((rows, 4 * LANES), F32), pltpu.VMEM((rows_c, 4 * LANES), F32),
                        pltpu.VMEM((rows, 4 * LANES), F32)],
        compiler_params=_cparams("parallel"),
        name="s5_scan",
    )(u3, uc3, wm, wp, qt, a16)


def _dft_kernel(c_ref, s_ref, gc_ref, gs_ref, o_ref, *, scale):
    acc = jnp.dot(c_ref[...], gc_ref[...], preferred_element_type=F32)
    acc = acc - jnp.dot(s_ref[...], gs_ref[...], preferred_element_type=F32)
    o_ref[...] = (acc * scale).astype(o_ref.dtype)


def _seq_dft(gc, gs, scale, tk):
    bsz, length, fw = gc.shape
    cm, sm = _dft_mats(length)
    cm = jnp.asarray(cm, BF16)
    sm = jnp.asarray(sm, BF16)
    return pl.pallas_call(
        functools.partial(_dft_kernel, scale=scale),
        grid=(bsz, length // tk),
        in_specs=[pl.BlockSpec((tk, length), lambda b, k: (k, 0)),
                  pl.BlockSpec((tk, length), lambda b, k: (k, 0)),
                  pl.BlockSpec((None, length, fw), lambda b, k: (b, 0, 0)),
                  pl.BlockSpec((None, length, fw), lambda b, k: (b, 0, 0))],
        out_specs=pl.BlockSpec((None, tk, fw), lambda b, k: (b, k, 0)),
        out_shape=jax.ShapeDtypeStruct((bsz, length, fw), BF16),
        compiler_params=_cparams("parallel", "arbitrary"),
        name="seq_dft",
    )(cm, sm, gc, gs)


def _outproj_kernel(y_ref, u_ref, fo_ref, x_ref, pos_ref, mods_ref, dsk_ref, wg_ref, bg_ref,
                    wo1_ref, wo2_ref, g2_ref, x1_ref, h2_ref, perm_s, *, d):
    tl = x_ref.shape[0]
    ncl = tl // CHUNK
    y = _permute_rows(y_ref[...].reshape(tl, y_ref.shape[-1]), perm_s, CHUNK, ncl)
    u = u_ref[...]
    y = y + dsk_ref[...] * u
    y = jax.nn.gelu(y)
    gate = jax.nn.sigmoid(jnp.dot(y.astype(BF16), wg_ref[...], preferred_element_type=F32) + bg_ref[...])
    s5o = (y * gate).astype(BF16)
    mix = jnp.dot(s5o, wo1_ref[...], preferred_element_type=F32)
    mix = mix + jnp.dot(fo_ref[...], wo2_ref[...], preferred_element_type=F32)
    m = mods_ref[...]
    g1 = m[:, 2 * d:3 * d]
    sh2 = m[:, 3 * d:4 * d]
    sc2 = m[:, 4 * d:5 * d]
    x1 = x_ref[...] + pos_ref[...] + g1 * mix
    x1_ref[...] = x1
    inv = lax.rsqrt(jnp.mean(x1 * x1, axis=-1, keepdims=True) + RMS_EPS)
    h2_ref[...] = ((x1 * inv) * g2_ref[...] * (1.0 + sc2) + sh2).astype(h2_ref.dtype)


def _outproj(y, u, fo, x, pos, mods3, d_skip, w_glu, b_glu, w_out, norm2_g, tl):
    bsz, length, d = x.shape
    sw = d_skip.shape[-1]
    fw = fo.shape[-1]
    tok = lambda wd: pl.BlockSpec((None, tl, wd), lambda b, i: (b, i, 0))
    stp = pl.BlockSpec((CHUNK, tl // CHUNK, sw), lambda b, i: (0, i, b))
    full = lambda r, c: pl.BlockSpec((r, c), lambda b, i: (0, 0))
    return pl.pallas_call(
        functools.partial(_outproj_kernel, d=d),
        grid=(bsz, length // tl),
        in_specs=[stp, tok(sw), tok(fw), tok(d),
                  pl.BlockSpec((tl, d), lambda b, i: (i, 0)),
                  pl.BlockSpec((None, 1, mods3.shape[2]), lambda b, i: (b, 0, 0)),
                  full(1, sw), full(sw, sw), full(1, sw), full(sw, d), full(fw, d), full(1, d)],
        out_specs=[tok(d), tok(d)],
        out_shape=[jax.ShapeDtypeStruct((bsz, length, d), F32),
                   jax.ShapeDtypeStruct((bsz, length, d), BF16)],
        scratch_shapes=[pltpu.VMEM((sw // LANES, tl, LANES), F32)],
        compiler_params=_cparams("parallel", "arbitrary"),
        name="outproj",
    )(y, u, fo, x, pos, mods3, d_skip.reshape(1, sw), w_glu.astype(BF16), b_glu.reshape(1, sw),
      w_out[:sw].astype(BF16), w_out[sw:].astype(BF16), norm2_g.reshape(1, d))


def _wqs_kernel(sk_ref, wq_ref, o_ref):
    nt = (((1,), (1,)), ((), ()))
    o_ref[...] = lax.dot_general(sk_ref[...], wq_ref[...], nt,
                                 preferred_element_type=F32, precision=HIGHEST)


def _fold_query_keys(w_q, sub_keys):
    d = w_q.shape[0]
    nh, _, nk, half = sub_keys.shape
    units = 2 * nh
    out = pl.pallas_call(
        _wqs_kernel,
        grid=(units,),
        in_specs=[pl.BlockSpec((None, None, nk, half), lambda u: (u % nh, u // nh, 0, 0)),
                  pl.BlockSpec((d, half), lambda u: (0, (u % nh) * 2 + u // nh))],
        out_specs=pl.BlockSpec((None, nk, d), lambda u: (u, 0, 0)),
        out_shape=jax.ShapeDtypeStruct((units, nk, d), F32),
        compiler_params=_cparams("arbitrary"),
        name="fold_query_keys",
    )(sub_keys, w_q)
    return out.transpose(1, 0, 2).reshape(nk * units, d).astype(BF16)


def _oddeven_merge_sort_pairs(n):
    pairs = []
    p = 1
    while p < n:
        k = p
        while k >= 1:
            for j in range(k % p, n - k, 2 * k):
                for i in range(min(k, n - j - k)):
                    if (i + j) // (2 * p) == (i + j + k) // (2 * p):
                        pairs.append((i + j, i + j + k))
            k //= 2
        p *= 2
    return pairs


_SORT16 = _oddeven_merge_sort_pairs(PEER_TOPK)


def _sort_desc(v):
    v = list(v)
    for i, j in _SORT16:
        hi = jnp.maximum(v[i], v[j])
        lo = jnp.minimum(v[i], v[j])
        v[i], v[j] = hi, lo
    return v


def _merge_top(a, b):
    n = len(a)
    c = [jnp.maximum(a[i], b[n - 1 - i]) for i in range(n)]
    k = n // 2
    while k >= 1:
        for i in range(n):
            if (i & k) == 0:
                hi = jnp.maximum(c[i], c[i + k])
                lo = jnp.minimum(c[i], c[i + k])
                c[i], c[i + k] = hi, lo
        k //= 2
    return c


def _top_sorted(vals):
    k = PEER_TOPK
    runs = [_sort_desc(vals[i:i + k]) for i in range(0, len(vals), k)]
    while len(runs) > 1:
        runs = [_merge_top(runs[i], runs[i + 1]) for i in range(0, len(runs), 2)]
    return runs[0]


def _cand_cells():
    k = PEER_TOPK
    return [(r0, r1) for r0 in range(k) for r1 in range(k) if (r0 + 1) * (r1 + 1) <= k]


_CELLS = _cand_cells()


def _select_block(s0, s1):
    k = PEER_TOPK
    shape = s0[0].shape
    one = jnp.ones(shape, F32)
    zero = jnp.zeros(shape, F32)
    sv0 = _top_sorted(s0)
    sv1 = _top_sorted(s1)
    cand = {c: sv0[c[0]] + sv1[c[1]] for c in _CELLS}
    cnt = {}
    for c in _CELLS:
        cnt[c] = float(sum(1 for o in _CELLS if o != c and o[0] <= c[0] and o[1] <= c[1]))
    acc = {c: None for c in _CELLS}
    for a_i, ca in enumerate(_CELLS):
        for cb in _CELLS[a_i + 1:]:
            if ca[0] <= cb[0] and ca[1] <= cb[1]:
                continue
            first = cand[ca] >= cand[cb]
            ib = jnp.where(first, one, zero)
            acc[cb] = ib if acc[cb] is None else acc[cb] + ib
            ia = one - ib
            acc[ca] = ia if acc[ca] is None else acc[ca] + ia
    e0 = [jnp.exp(sv0[r] - sv0[0]) for r in range(k)]
    e1 = [jnp.exp(sv1[r] - sv1[0]) for r in range(k)]
    nsel = [zero] * k
    zsum = zero
    for c in _CELLS:
        tot = acc[c] + cnt[c] if acc[c] is not None else jnp.full(shape, cnt[c], F32)
        sel = jnp.where(tot < float(k), one, zero)
        nsel[c[0]] = nsel[c[0]] + sel
        zsum = zsum + sel * (e0[c[0]] * e1[c[1]])
    inv_z = 1.0 / zsum
    n_l, ex0_l, r_l, ex1_l = [], [], [], []
    for e in range(len(s0)):
        n_e = zero
        r_e = jnp.full(shape, float(k), F32)
        for r in reversed(range(k)):
            n_e = jnp.where(s0[e] >= sv0[r], nsel[r], n_e)
            r_e = jnp.where(s1[e] >= sv1[r], float(r), r_e)
        n_l.append(n_e)
        r_l.append(r_e)
        ex0_l.append(jnp.exp(s0[e] - sv0[0]) * inv_z)
        ex1_l.append(jnp.exp(s1[e] - sv1[0]))
    return n_l, ex0_l, r_l, ex1_l


def _select_kernel(wqs_ref, h2_ref, n0_ref, ex0_ref, r1_ref, ex1_ref, s_s, *, tm):
    nh = PEER_HEADS
    nk = PEER_KEYS
    units = 2 * nh
    nt = (((1,), (1,)), ((), ()))
    s_s[...] = lax.dot_general(wqs_ref[...], h2_ref[...], nt, preferred_element_type=F32)

    def block(j, carry):
        c0 = pl.multiple_of(j * LANES, LANES)
        cols = pl.ds(c0, LANES)
        s0 = [s_s[e * units:e * units + nh, cols] for e in range(nk)]
        s1 = [s_s[e * units + nh:(e + 1) * units, cols] for e in range(nk)]
        n_l, ex0_l, r_l, ex1_l = _select_block(s0, s1)
        for e in range(nk):
            n0_ref[e, :, cols] = n_l[e]
            ex0_ref[e, :, cols] = ex0_l[e]
            for h in range(nh):
                r1_ref[h, pl.ds(e, 1), cols] = r_l[e][h:h + 1, :]
                ex1_ref[h, pl.ds(e, 1), cols] = ex1_l[e][h:h + 1, :]
        return carry

    lax.fori_loop(0, tm // LANES, block, 0)


def _peer_select(wqs, h2, tm):
    tokens, d = h2.shape
    nh = PEER_HEADS
    nk = PEER_KEYS
    rows = wqs.shape[0]
    o_spec = pl.BlockSpec((nk, nh, tm), lambda i: (0, 0, i))
    o_shape = jax.ShapeDtypeStruct((nk, nh, tokens), F32)
    t_spec = pl.BlockSpec((nh, nk, tm), lambda i: (0, 0, i))
    t_shape = jax.ShapeDtypeStruct((nh, nk, tokens), F32)
    return pl.pallas_call(
        functools.partial(_select_kernel, tm=tm),
        grid=(tokens // tm,),
        in_specs=[pl.BlockSpec((rows, d), lambda i: (0, 0)),
                  pl.BlockSpec((tm, d), lambda i: (i, 0))],
        out_specs=[o_spec, o_spec, t_spec, t_spec],
        out_shape=[o_shape, o_shape, t_shape, t_shape],
        scratch_shapes=[pltpu.VMEM((rows, tm), F32)],
        compiler_params=_cparams("parallel"),
        name="peer_select",
    )(wqs, h2)


def _experts_kernel(u_ref, vt_ref, h2_ref, n0_ref, ex0_ref, r1_ref, ex1_ref, x1_ref,
                    mods_ref, fg_ref, o_ref, acc_s, r1_s, ex1_s, w_s, *, d, te, tm, tq):
    nh = PEER_HEADS
    nk = PEER_KEYS
    j = pl.program_id(1)
    nt = (((1,), (1,)), ((), ()))

    @pl.when(j == 0)
    def _():
        acc_s[...] = jnp.zeros_like(acc_s)
        r1_s[...] = r1_ref[...].astype(BF16)
        ex1_s[...] = ex1_ref[...].astype(BF16)

    u = u_ref[...]
    vt = vt_ref[...]
    nq = tm // tq

    def scores(q):
        return lax.dot_general(u, h2_ref[q * tq:(q + 1) * tq, :], nt, preferred_element_type=F32)

    def weights(q):
        cs = slice(q * tq, (q + 1) * tq)
        for el in range(te // nk):
            n0 = n0_ref[el, :, cs].astype(BF16)
            ex0 = ex0_ref[el, :, cs].astype(BF16)
            wgt = jnp.zeros((nk, tq), BF16)
            for h in range(nh):
                ex1 = ex1_s[h, :, cs]
                keep = r1_s[h, :, cs] < n0[h:h + 1, :]
                wgt = wgt + jnp.where(keep, ex1, jnp.zeros_like(ex1)) * ex0[h:h + 1, :]
            w_s[q % 2, el * nk:(el + 1) * nk, :] = wgt

    weights(0)
    a_next = scores(0)
    for q in range(nq):
        cs = slice(q * tq, (q + 1) * tq)
        a = a_next
        if q + 1 < nq:
            a_next = scores(q + 1)
        parts = [jax.nn.gelu(a[el * nk:(el + 1) * nk, :].astype(BF16)) * w_s[q % 2, el * nk:(el + 1) * nk, :]
                 for el in range(te // nk)]
        if q + 1 < nq:
            weights(q + 1)
        wa = jnp.concatenate(parts, axis=0)
        acc_s[:, cs] += jnp.dot(vt, wa, preferred_element_type=F32)

    @pl.when(j == pl.num_programs(1) - 1)
    def _():
        peer = acc_s[...].T
        m = mods_ref[...]
        g2 = m[:, 5 * d:6 * d]
        xo = x1_ref[...] + g2 * peer
        inv = lax.rsqrt(jnp.mean(xo * xo, axis=-1, keepdims=True) + RMS_EPS)
        o_ref[...] = (xo * inv) * fg_ref[...]


def _peer_experts(u_bf, vt_bf, h2, n0, ex0, r1, ex1, x1, mods3, final_g, tm, te, tq):
    bsz, length, d = x1.shape
    tokens = bsz * length
    ne = u_bf.shape[0]
    nh = PEER_HEADS
    nk = PEER_KEYS
    tpb = length // tm
    once = pl.Buffered(1)
    sel0 = pl.BlockSpec((te // nk, nh, tm), lambda i, j: (j, 0, i))
    sel1 = pl.BlockSpec((nh, nk, tm), lambda i, j: (0, 0, i), pipeline_mode=once)
    return pl.pallas_call(
        functools.partial(_experts_kernel, d=d, te=te, tm=tm, tq=tq),
        grid=(tokens // tm, ne // te),
        in_specs=[pl.BlockSpec((te, d), lambda i, j: (j, 0)),
                  pl.BlockSpec((d, te), lambda i, j: (0, j)),
                  pl.BlockSpec((tm, d), lambda i, j: (i, 0), pipeline_mode=once),
                  sel0, sel0, sel1, sel1,
                  pl.BlockSpec((None, tm, d), lambda i, j: (i // tpb, i % tpb, 0), pipeline_mode=once),
                  pl.BlockSpec((None, 1, mods3.shape[2]), lambda i, j: (i // tpb, 0, 0)),
                  pl.BlockSpec((1, d), lambda i, j: (0, 0))],
        out_specs=pl.BlockSpec((None, tm, d), lambda i, j: (i // tpb, i % tpb, 0)),
        out_shape=jax.ShapeDtypeStruct((bsz, length, d), F32),
        scratch_shapes=[pltpu.VMEM((d, tm), F32),
                        pltpu.VMEM((nh, nk, tm), BF16), pltpu.VMEM((nh, nk, tm), BF16),
                        pltpu.VMEM((2, te, tq), BF16)],
        compiler_params=_cparams("parallel", "arbitrary"),
        name="peer_experts",
    )(u_bf, vt_bf, h2, n0, ex0, r1, ex1, x1, mods3, final_g.reshape(1, d))


def kernel(x, c, ctx, c_ctx, w_mod, b_mod, norm1_g, norm2_g, w_in, w_out, s5_lam_re, s5_lam_im,
           s5_log_dt, s5_b_re, s5_b_im, s5_c_re, s5_c_im, s5_d, w_glu, b_glu, w_fourier,
           peer_w_q, peer_sub_keys, peer_u, peer_v, final_g):
    bsz, length, d = x.shape
    clen = ctx.shape[1]
    sw = s5_d.shape[-1]
    fw = w_in.shape[-1] - sw
    assert w_mod.shape[0] == 1, "single layer: the context stream only feeds the S5 initial states"
    assert length % CHUNK == 0 and clen % CHUNK == 0 and bsz % 8 == 0

    tl = min(512, length)
    tlc = min(512, clen)
    tm = min(512, length)
    te = 1024

    pos = jnp.asarray(_pos_embed(length // GRID_W, d))

    mrows = -(-(bsz + 1) // 8) * 8
    c_rows = jnp.concatenate([c, c_ctx[None, :], jnp.zeros((mrows - bsz - 1, d), F32)], axis=0)
    mods = _modulation(c_rows, w_mod[0], b_mod[0])
    mods3 = mods.reshape(mrows, 1, N_MOD * d)

    wfc, wfs = _fourier_fold(w_in[0][:, sw:], w_fourier[0])
    w_all = jnp.concatenate([w_in[0][:, :sw], wfc, wfs], axis=1).astype(BF16)

    u3, u, gc, gs = _inproj(x, pos, mods3, None, norm1_g[0], w_all, (sw, fw, fw), (F32, BF16, BF16), tl)
    (uc3,) = _inproj(ctx, None, mods3, bsz, norm1_g[0], w_all[:, :sw], (sw,), (), tlc)

    wm, wp, qt, a16 = _s5_prep(s5_lam_re[0], s5_lam_im[0], s5_log_dt[0], s5_b_re[0], s5_b_im[0],
                               s5_c_re[0], s5_c_im[0])
    y3 = _s5_scan(u3.reshape(CHUNK, -1, sw), uc3.reshape(CHUNK, -1, sw), wm, wp, qt, a16, bsz)
    y = y3.reshape(u3.shape)

    fo = _seq_dft(gc, gs, 1.0 / math.sqrt(length * (fw // FOURIER_GROUPS)), min(512, length))

    x1, h2 = _outproj(y, u, fo, x, pos, mods3, s5_d[0], w_glu[0], b_glu[0], w_out[0], norm2_g[0], tl)

    wqs = _fold_query_keys(peer_w_q[0], peer_sub_keys[0])
    h2f = h2.reshape(bsz * length, d)
    n0, ex0, r1, ex1 = _peer_select(wqs, h2f, tm)
    u_bf = peer_u[0].astype(BF16)
    vt_bf = peer_v[0].astype(BF16).T
    tme = min(1024, length)
    return _peer_experts(u_bf, vt_bf, h2f, n0, ex0, r1, ex1, x1, mods3, final_g, tme, te, min(256, tme))
```

```python
import functools
import math

import numpy as np
import jax
import jax.numpy as jnp
from jax import lax
from jax.experimental import pallas as pl
from jax.experimental.pallas import tpu as pltpu

F32 = jnp.float32
BF16 = jnp.bfloat16
HIGHEST = lax.Precision.HIGHEST

GRID_W = 64
POS_BASE = 10000.0
RMS_EPS = 1e-6
N_MOD = 6
S5_GROUP = 16
S5_STATE = 64
CHUNK = 16
FOURIER_GROUPS = 4
PEER_HEADS = 8
PEER_KEYS = 128
PEER_TOPK = 16
LANES = 128
VMEM_LIMIT = 56 * 1024 * 1024


def _cparams(*sem):
    return pltpu.CompilerParams(dimension_semantics=sem, vmem_limit_bytes=VMEM_LIMIT)


def _pos_embed(rows, dim):
    quarter = dim // 4
    omega = 1.0 / (POS_BASE ** (jnp.arange(quarter, dtype=F32) / quarter))
    r = jnp.repeat(jnp.arange(rows, dtype=F32), GRID_W)[:, None] * omega
    col = jnp.tile(jnp.arange(GRID_W, dtype=F32), rows)[:, None] * omega
    return jnp.concatenate([jnp.sin(r), jnp.cos(r), jnp.sin(col), jnp.cos(col)], axis=-1)


def _dft_mats(n):
    k = np.arange(n, dtype=np.int64)
    ang = 2.0 * np.pi * ((k[:, None] * k[None, :]) % n).astype(np.float64) / n
    return np.cos(ang), np.sin(ang)


def _mod_kernel(c_ref, w_ref, b_ref, o_ref):
    c = c_ref[...]
    a = c * jax.nn.sigmoid(c)
    o_ref[...] = jnp.dot(a.astype(BF16), w_ref[...].astype(BF16),
                         preferred_element_type=F32) + b_ref[...]


def _modulation(c_rows, w_mod, b_mod):
    rows, d = c_rows.shape
    n = w_mod.shape[1]
    tn = d
    return pl.pallas_call(
        _mod_kernel,
        grid=(n // tn,),
        in_specs=[pl.BlockSpec((rows, d), lambda j: (0, 0)),
                  pl.BlockSpec((d, tn), lambda j: (0, j)),
                  pl.BlockSpec((1, tn), lambda j: (0, j))],
        out_specs=pl.BlockSpec((rows, tn), lambda j: (0, j)),
        out_shape=jax.ShapeDtypeStruct((rows, n), F32),
        compiler_params=_cparams("arbitrary"),
        name="modulation",
    )(c_rows, w_mod, b_mod.reshape(1, n))


def _gather_rows(val, perm_s, starts, n_out, stride):
    width = val.shape[1]
    nlt = width // LANES
    for k in range(nlt):
        perm_s[k] = val[:, k * LANES:(k + 1) * LANES]
    blocks = []
    for j in starts:
        blocks.append(jnp.concatenate(
            [perm_s[k, pl.ds(j, n_out, stride=stride), :] for k in range(nlt)], axis=1))
    return jnp.concatenate(blocks, axis=0)


def _inproj_kernel(*refs, has_pos, d, widths):
    if has_pos:
        x_ref, pos_ref, mods_ref, g_ref, w_ref = refs[:5]
        rest = refs[5:]
    else:
        x_ref, mods_ref, g_ref, w_ref = refs[:4]
        rest = refs[4:]
    outs, perm_s = rest[:-1], rest[-1]
    bsz, tlb, _ = x_ref.shape
    x = x_ref[...]
    if has_pos:
        x = x + pos_ref[...][None]
    inv = lax.rsqrt(jnp.mean(x * x, axis=-1, keepdims=True) + RMS_EPS)
    m = mods_ref[...]
    sh = m[:, :, 0:d]
    sc = m[:, :, d:2 * d]
    h = (x * inv) * g_ref[...] * (1.0 + sc) + sh
    h = h.reshape(bsz * tlb, d).astype(BF16)
    z = jnp.dot(h, w_ref[...], preferred_element_type=F32)
    ncl = tlb // CHUNK
    starts = [cl * CHUNK + s for s in range(CHUNK) for cl in range(ncl)]
    outs[0][...] = _gather_rows(z[:, 0:widths[0]], perm_s, starts, bsz, tlb).reshape(outs[0].shape)
    off = widths[0]
    for o_ref, wd in zip(outs[1:], widths[1:]):
        o_ref[...] = z[:, off:off + wd].astype(o_ref.dtype).reshape(o_ref.shape)
        off += wd


def _inproj(x, pos, mods3, mod_row, g, w, widths, dtypes, tl):
    bsz, length, d = x.shape
    has_pos = pos is not None
    nw = w.shape[1]
    if mod_row is None:
        mod_spec = pl.BlockSpec((bsz, 1, mods3.shape[2]), lambda i: (0, 0, 0))
    else:
        mod_spec = pl.BlockSpec((1, 1, mods3.shape[2]), lambda i: (mod_row, 0, 0))
    in_specs = [pl.BlockSpec((bsz, tl, d), lambda i: (0, i, 0))]
    args = [x]
    if has_pos:
        in_specs.append(pl.BlockSpec((tl, d), lambda i: (i, 0)))
        args.append(pos)
    in_specs += [mod_spec,
                 pl.BlockSpec((1, d), lambda i: (0, 0)),
                 pl.BlockSpec((d, nw), lambda i: (0, 0))]
    args += [mods3, g.reshape(1, d), w]
    ncl = tl // CHUNK
    out_specs = [pl.BlockSpec((CHUNK, ncl * bsz, widths[0]), lambda i: (0, i, 0))]
    out_shape = [jax.ShapeDtypeStruct((CHUNK, (length // CHUNK) * bsz, widths[0]), F32)]
    out_specs += [pl.BlockSpec((bsz, tl, wd), lambda i: (0, i, 0)) for wd in widths[1:]]
    out_shape += [jax.ShapeDtypeStruct((bsz, length, wd), dt) for wd, dt in zip(widths[1:], dtypes)]
    return pl.pallas_call(
        functools.partial(_inproj_kernel, has_pos=has_pos, d=d, widths=tuple(widths)),
        grid=(length // tl,),
        in_specs=in_specs,
        out_specs=out_specs,
        out_shape=out_shape,
        scratch_shapes=[pltpu.VMEM((widths[0] // LANES, bsz * tl, LANES), F32)],
        compiler_params=_cparams("parallel"),
        name="inproj_pos" if has_pos else "inproj_ctx",
    )(*args)


def _ffold_kernel(win_ref, wf_ref, cc_ref, sc_ref, oc_ref, os_ref):
    wf = wf_ref[...]
    cw = jnp.dot(cc_ref[...], wf, preferred_element_type=F32, precision=HIGHEST)
    sw = jnp.dot(sc_ref[...], wf, preferred_element_type=F32, precision=HIGHEST)
    w = win_ref[...]
    oc_ref[...] = jnp.dot(w, cw, preferred_element_type=F32, precision=HIGHEST)
    os_ref[...] = jnp.dot(w, sw, preferred_element_type=F32, precision=HIGHEST)


def _fourier_fold(w_in_f, w_fourier):
    d, fw = w_in_f.shape
    ng, gc, _ = w_fourier.shape
    cc, sc = _dft_mats(gc)
    cc = jnp.asarray(cc, F32)
    sc = jnp.asarray(sc, F32)
    return pl.pallas_call(
        _ffold_kernel,
        grid=(ng,),
        in_specs=[pl.BlockSpec((d, gc), lambda g: (0, g)),
                  pl.BlockSpec((None, gc, gc), lambda g: (g, 0, 0)),
                  pl.BlockSpec((gc, gc), lambda g: (0, 0)),
                  pl.BlockSpec((gc, gc), lambda g: (0, 0))],
        out_specs=[pl.BlockSpec((d, gc), lambda g: (0, g)),
                   pl.BlockSpec((d, gc), lambda g: (0, g))],
        out_shape=[jax.ShapeDtypeStruct((d, fw), F32), jax.ShapeDtypeStruct((d, fw), F32)],
        compiler_params=_cparams("arbitrary"),
        name="fourier_fold",
    )(w_in_f, w_fourier, cc, sc)


def _s5prep_kernel(lr_ref, li_ref, ldt_ref, btr_ref, bti_ref, cr_ref, ci_ref,
                   wm_ref, wp_ref, qt_ref, a16_ref,
                   pwr_s, pwi_s, cef_re, cef_im, ceb_re, ceb_im):
    t = CHUNK
    gp = S5_GROUP
    ns = S5_STATE
    erow = lax.broadcasted_iota(jnp.int32, (24, LANES), 0).astype(F32)
    lane = lax.broadcasted_iota(jnp.int32, (1, LANES), 1)
    gmask = [(lane < ns).astype(F32), (lane >= ns).astype(F32)]
    bbt = []
    for dr in range(2):
        lr = lr_ref[dr]
        li = li_ref[dr]
        dt = jnp.exp(ldt_ref[dr])
        mag = jnp.exp(lr * dt * erow)
        ang = li * dt * erow
        pwr_s[dr] = mag * jnp.cos(ang)
        pwi_s[dr] = mag * jnp.sin(ang)
        ar = pwr_s[dr, 1:2, :]
        ai = pwi_s[dr, 1:2, :]
        den = lr * lr + li * li
        cr = ((ar - 1.0) * lr + ai * li) / den
        ci = (ai * lr - (ar - 1.0) * li) / den
        btr = btr_ref[dr]
        bti = bti_ref[dr]
        bbt.append((cr * btr - ci * bti, cr * bti + ci * btr))
    gw = 2 * gp
    for blk in range(t + 1):
        ef = blk
        eb = t - blk
        prf = pwr_s[0, ef:ef + 1, :]
        pif = pwi_s[0, ef:ef + 1, :]
        prb = pwr_s[1, eb:eb + 1, :]
        pib = pwi_s[1, eb:eb + 1, :]
        for gl in range(2):
            m = gmask[gl]
            r0 = blk * gw + gl * gp
            cef_re[r0:r0 + gp, :] = (cr_ref[0] * prf - ci_ref[0] * pif) * m
            cef_im[r0:r0 + gp, :] = -(cr_ref[0] * pif + ci_ref[0] * prf) * m
            ceb_re[r0:r0 + gp, :] = (cr_ref[1] * prb - ci_ref[1] * pib) * m
            ceb_im[r0:r0 + gp, :] = -(cr_ref[1] * pib + ci_ref[1] * prb) * m
    w = t * gw
    lane_w = lax.broadcasted_iota(jnp.int32, (gw, w), 1)
    nt = (((1,), (1,)), ((), ()))
    btm = [[jnp.concatenate([bbt[dr][ri] * gmask[0], bbt[dr][ri] * gmask[1]], axis=0) for ri in range(2)]
           for dr in range(2)]
    btf = jnp.concatenate(btm[0], axis=1)
    btb = jnp.concatenate(btm[1], axis=1)
    cef = jnp.concatenate([cef_re[0:w, :], cef_im[0:w, :]], axis=1)
    ceb = jnp.concatenate([ceb_re[gw:gw + w, :], ceb_im[gw:gw + w, :]], axis=1)
    kf = lax.dot_general(btf, cef, nt, preferred_element_type=F32, precision=HIGHEST)
    kb = lax.dot_general(btb, ceb, nt, preferred_element_type=F32, precision=HIGHEST)
    for s in range(t):
        fwd = kf if s == 0 else pltpu.roll(kf, gw * s, 1)
        fwd = jnp.where(lane_w >= gw * s, fwd, 0.0)
        sh = (gw * (s + 1)) % w
        bwd = kb if sh == 0 else pltpu.roll(kb, sh, 1)
        bwd = jnp.where(lane_w < gw * (s + 1), bwd, 0.0)
        wm_ref[s * gw:(s + 1) * gw, :] = (fwd + bwd).astype(wm_ref.dtype)
        ef = t - 1 - s
        pr = pwr_s[0, ef:ef + 1, :]
        pi = pwi_s[0, ef:ef + 1, :]
        r0 = s * gw
        wp_ref[r0:r0 + gw, 0:LANES] = (pr * btm[0][0] - pi * btm[0][1]).astype(wp_ref.dtype)
        wp_ref[r0:r0 + gw, LANES:2 * LANES] = (pr * btm[0][1] + pi * btm[0][0]).astype(wp_ref.dtype)
        pr = pwr_s[1, s:s + 1, :]
        pi = pwi_s[1, s:s + 1, :]
        wp_ref[r0:r0 + gw, 2 * LANES:3 * LANES] = (pr * btm[1][0] - pi * btm[1][1]).astype(wp_ref.dtype)
        wp_ref[r0:r0 + gw, 3 * LANES:4 * LANES] = (pr * btm[1][1] + pi * btm[1][0]).astype(wp_ref.dtype)
    qt_ref[:, 0:LANES] = cef_re[gw:gw + w, :].astype(qt_ref.dtype)
    qt_ref[:, LANES:2 * LANES] = cef_im[gw:gw + w, :].astype(qt_ref.dtype)
    qt_ref[:, 2 * LANES:3 * LANES] = ceb_re[0:w, :].astype(qt_ref.dtype)
    qt_ref[:, 3 * LANES:4 * LANES] = ceb_im[0:w, :].astype(qt_ref.dtype)
    a16_ref[0:1, :] = pwr_s[0, t:t + 1, :]
    a16_ref[1:2, :] = pwi_s[0, t:t + 1, :]
    a16_ref[2:3, :] = pwr_s[1, t:t + 1, :]
    a16_ref[3:4, :] = pwi_s[1, t:t + 1, :]


def _s5_prep(lam_re, lam_im, log_dt, b_re, b_im, c_re, c_im):
    _, g, n = lam_re.shape
    p = b_re.shape[-1]
    npair = g // 2
    lanes = 2 * n
    lr = lam_re.reshape(2, npair, 1, lanes)
    li = lam_im.reshape(2, npair, 1, lanes)
    ldt = jnp.repeat(log_dt, n, axis=-1).reshape(2, npair, 1, lanes)

    def pair_rows(a):
        r = a.shape[2]
        return a.reshape(2, npair, 2, r, n).transpose(0, 1, 3, 2, 4).reshape(2, npair, r, lanes)

    btr = pair_rows(jnp.swapaxes(b_re, -1, -2))
    bti = pair_rows(jnp.swapaxes(b_im, -1, -2))
    cr = pair_rows(c_re)
    ci = pair_rows(c_im)
    w = CHUNK * 2 * p
    vec = pl.BlockSpec((2, None, 1, lanes), lambda i: (0, i, 0, 0))
    mat = pl.BlockSpec((2, None, p, lanes), lambda i: (0, i, 0, 0))
    wsq = pl.BlockSpec((None, w, w), lambda i: (i, 0, 0))
    ce_rows = w + 2 * p
    return pl.pallas_call(
        _s5prep_kernel,
        grid=(npair,),
        in_specs=[vec, vec, vec, mat, mat, mat, mat],
        out_specs=[wsq, wsq, wsq, pl.BlockSpec((None, 4, lanes), lambda i: (i, 0, 0))],
        out_shape=[jax.ShapeDtypeStruct((npair, w, w), BF16),
                   jax.ShapeDtypeStruct((npair, w, w), BF16),
                   jax.ShapeDtypeStruct((npair, w, w), BF16),
                   jax.ShapeDtypeStruct((npair, 4, lanes), F32)],
        scratch_shapes=[pltpu.VMEM((2, 24, lanes), F32), pltpu.VMEM((2, 24, lanes), F32),
                        pltpu.VMEM((ce_rows, lanes), F32), pltpu.VMEM((ce_rows, lanes), F32),
                        pltpu.VMEM((ce_rows, lanes), F32), pltpu.VMEM((ce_rows, lanes), F32)],
        compiler_params=_cparams("arbitrary"),
        name="s5_prep",
    )(lr, li, ldt, btr, bti, cr, ci)


def _s5_gather(ref, pr):
    gw = 2 * S5_GROUP
    ppt = LANES // gw
    seg = lax.broadcasted_iota(jnp.int32, (1, LANES), 1) // gw
    tiles = []
    for m in range(CHUNK // ppt):
        tile = None
        for i in range(ppt):
            x = ref[m * ppt + i]
            sh = (gw * (i - pr)) % LANES
            if sh:
                x = pltpu.roll(x, sh, 1)
            tile = x if tile is None else jnp.where(seg == i, x, tile)
        tiles.append(tile)
    return jnp.concatenate(tiles, axis=1).astype(BF16)


def _s5_kernel(u_ref, uc_ref, wm_ref, wp_ref, qt_ref, a16_ref, y_ref, st_s, stc_s, fg_s,
               *, bsz, n_chunk, n_chunk_ctx):
    gw = 2 * S5_GROUP
    ppt = LANES // gw
    for pr in range(ppt):
        _s5_pair(u_ref, uc_ref, wm_ref, wp_ref, qt_ref, a16_ref, y_ref, st_s, stc_s, fg_s, pr,
                 bsz=bsz, n_chunk=n_chunk, n_chunk_ctx=n_chunk_ctx)


def _s5_pair(u_ref, uc_ref, wm_ref, wp_ref, qt_ref, a16_ref, y_ref, st_s, stc_s, fg_s, pr,
             *, bsz, n_chunk, n_chunk_ctx):
    gw = 2 * S5_GROUP
    ppt = LANES // gw
    seg = lax.broadcasted_iota(jnp.int32, (1, LANES), 1) // gw
    u = _s5_gather(u_ref, pr)
    wp = wp_ref[pr]
    stc_s[...] = jnp.dot(_s5_gather(uc_ref, pr), wp, preferred_element_type=F32)
    st_s[...] = jnp.dot(u, wp, preferred_element_type=F32)
    a = a16_ref[pr]
    afr = jnp.broadcast_to(a[0:1, :], (bsz, LANES))
    afi = jnp.broadcast_to(a[1:2, :], (bsz, LANES))
    abr = jnp.broadcast_to(a[2:3, :], (bsz, LANES))
    abi = jnp.broadcast_to(a[3:4, :], (bsz, LANES))
    zero = jnp.zeros((bsz, LANES), F32)

    def step(ar, ai, hr, hi, br, bi):
        return ar * hr - ai * hi + br, ar * hi + ai * hr + bi

    fr, fi = zero, zero
    for c in range(n_chunk_ctx):
        blk = stc_s[c * bsz:(c + 1) * bsz, :]
        fr, fi = step(afr, afi, fr, fi, blk[:, 0:LANES], blk[:, LANES:2 * LANES])
    gr, gi = zero, zero
    for c in reversed(range(n_chunk_ctx)):
        blk = stc_s[c * bsz:(c + 1) * bsz, :]
        gr, gi = step(abr, abi, gr, gi, blk[:, 2 * LANES:3 * LANES], blk[:, 3 * LANES:4 * LANES])

    def scan_body(c, carry):
        hr, hi, kr, ki = carry
        rf = pl.multiple_of(c * bsz, bsz)
        rb = pl.multiple_of((n_chunk - 1 - c) * bsz, bsz)
        fg_s[pl.ds(rf, bsz), 0:LANES] = hr
        fg_s[pl.ds(rf, bsz), LANES:2 * LANES] = hi
        fg_s[pl.ds(rb, bsz), 2 * LANES:3 * LANES] = kr
        fg_s[pl.ds(rb, bsz), 3 * LANES:4 * LANES] = ki
        hr, hi = step(afr, afi, hr, hi, st_s[pl.ds(rf, bsz), 0:LANES], st_s[pl.ds(rf, bsz), LANES:2 * LANES])
        kr, ki = step(abr, abi, kr, ki, st_s[pl.ds(rb, bsz), 2 * LANES:3 * LANES],
                      st_s[pl.ds(rb, bsz), 3 * LANES:4 * LANES])
        return hr, hi, kr, ki

    lax.fori_loop(0, n_chunk, scan_body, (fr, fi, gr, gi), unroll=4)

    fg = fg_s[...].astype(BF16)
    nt = (((1,), (1,)), ((), ()))
    y = jnp.dot(u, wm_ref[pr], preferred_element_type=F32)
    y = y + lax.dot_general(fg, qt_ref[pr], nt, preferred_element_type=F32)
    for t in range(CHUNK):
        m, i = divmod(t, ppt)
        piece = y[:, m * LANES:(m + 1) * LANES]
        sh = (gw * (pr - i)) % LANES
        if sh:
            piece = pltpu.roll(piece, sh, 1)
        y_ref[t] = piece if pr == 0 else jnp.where(seg == pr, piece, y_ref[t])


def _s5_scan(u3, uc3, wm, wp, qt, a16, bsz):
    _, rows, width = u3.shape
    rows_c = uc3.shape[1]
    w = wm.shape[-1]
    ppt = LANES // (2 * S5_GROUP)
    once = pl.Buffered(1)
    wspec = pl.BlockSpec((ppt, w, w), lambda i: (i, 0, 0))
    return pl.pallas_call(
        functools.partial(_s5_kernel, bsz=bsz, n_chunk=rows // bsz, n_chunk_ctx=rows_c // bsz),
        grid=(width // LANES,),
        in_specs=[pl.BlockSpec((CHUNK, rows, LANES), lambda i: (0, 0, i), pipeline_mode=once),
                  pl.BlockSpec((CHUNK, rows_c, LANES), lambda i: (0, 0, i)),
                  wspec, wspec, wspec,
                  pl.BlockSpec((ppt, 4, LANES), lambda i: (i, 0, 0))],
        out_specs=pl.BlockSpec((CHUNK, rows, LANES), lambda i: (0, 0, i)),
        out_shape=jax.ShapeDtypeStruct((CHUNK, rows, width), F32),
        scratch_shapes=[pltpu.VMEM((rows, 4 * LANES), F32), pltpu.VMEM((rows_c, 4 * LANES), F32),
                        pltpu.VMEM((rows, 4 * LANES), F32)],
        compiler_params=_cparams("parallel"),
        name="s5_scan",
    )(u3, uc3, wm, wp, qt, a16)


def _dft_kernel(c_ref, s_ref, gc_ref, gs_ref, o_ref, *, scale):
    acc = jnp.dot(c_ref[...].astype(BF16), gc_ref[...], preferred_element_type=F32)
    acc = acc - jnp.dot(s_ref[...].astype(BF16), gs_ref[...], preferred_element_type=F32)
    o_ref[...] = (acc * scale).astype(o_ref.dtype)


def _seq_dft(gc, gs, scale, tk):
    bsz, length, fw = gc.shape
    cm, sm = _dft_mats(length)
    cm = jnp.asarray(cm, F32)
    sm = jnp.asarray(sm, F32)
    return pl.pallas_call(
        functools.partial(_dft_kernel, scale=scale),
        grid=(length // tk, bsz),
        in_specs=[pl.BlockSpec((tk, length), lambda k, b: (k, 0)),
                  pl.BlockSpec((tk, length), lambda k, b: (k, 0)),
                  pl.BlockSpec((None, length, fw), lambda k, b: (b, 0, 0)),
                  pl.BlockSpec((None, length, fw), lambda k, b: (b, 0, 0))],
        out_specs=pl.BlockSpec((None, tk, fw), lambda k, b: (b, k, 0)),
        out_shape=jax.ShapeDtypeStruct((bsz, length, fw), BF16),
        compiler_params=_cparams("parallel", "arbitrary"),
        name="seq_dft",
    )(cm, sm, gc, gs)


def _outproj_kernel(y_ref, u_ref, fo_ref, x_ref, pos_ref, mods_ref, dsk_ref, wg_ref, bg_ref,
                    wo1_ref, wo2_ref, g2_ref, x1_ref, h2_ref, perm_s, *, d):
    bsz, tlb, _ = x_ref.shape
    rows = bsz * tlb
    ncl = tlb // CHUNK
    sw = y_ref.shape[-1]
    starts = [cl * bsz + b for b in range(bsz) for cl in range(ncl)]
    y = _gather_rows(y_ref[...].reshape(rows, sw), perm_s, starts, CHUNK, ncl * bsz)
    u = _gather_rows(u_ref[...].reshape(rows, sw), perm_s, starts, CHUNK, ncl * bsz)
    y = y + dsk_ref[...] * u
    y = jax.nn.gelu(y)
    gate = jax.nn.sigmoid(jnp.dot(y.astype(BF16), wg_ref[...], preferred_element_type=F32) + bg_ref[...])
    s5o = (y * gate).astype(BF16)
    mix = jnp.dot(s5o, wo1_ref[...], preferred_element_type=F32)
    mix = mix + jnp.dot(fo_ref[...].reshape(rows, fo_ref.shape[-1]), wo2_ref[...],
                        preferred_element_type=F32)
    m = mods_ref[...]
    g1 = m[:, :, 2 * d:3 * d]
    sh2 = m[:, :, 3 * d:4 * d]
    sc2 = m[:, :, 4 * d:5 * d]
    x1 = x_ref[...] + pos_ref[...][None] + g1 * mix.reshape(bsz, tlb, d)
    x1_ref[...] = x1
    inv = lax.rsqrt(jnp.mean(x1 * x1, axis=-1, keepdims=True) + RMS_EPS)
    h2_ref[...] = ((x1 * inv) * g2_ref[...] * (1.0 + sc2) + sh2).astype(h2_ref.dtype)


def _outproj(y, u, fo, x, pos, mods3, d_skip, w_glu, b_glu, w_out, norm2_g, tl):
    bsz, length, d = x.shape
    sw = d_skip.shape[-1]
    fw = fo.shape[-1]
    tok = lambda wd: pl.BlockSpec((bsz, tl, wd), lambda i: (0, i, 0))
    stp = pl.BlockSpec((CHUNK, (tl // CHUNK) * bsz, sw), lambda i: (0, i, 0))
    full = lambda r, c: pl.BlockSpec((r, c), lambda i: (0, 0))
    return pl.pallas_call(
        functools.partial(_outproj_kernel, d=d),
        grid=(length // tl,),
        in_specs=[stp, stp, tok(fw), tok(d),
                  pl.BlockSpec((tl, d), lambda i: (i, 0)),
                  pl.BlockSpec((bsz, 1, mods3.shape[2]), lambda i: (0, 0, 0)),
                  full(1, sw), full(sw, sw), full(1, sw), full(sw, d), full(fw, d), full(1, d)],
        out_specs=[tok(d), tok(d)],
        out_shape=[jax.ShapeDtypeStruct((bsz, length, d), F32),
                   jax.ShapeDtypeStruct((bsz, length, d), BF16)],
        scratch_shapes=[pltpu.VMEM((sw // LANES, bsz * tl, LANES), F32)],
        compiler_params=_cparams("parallel"),
        name="outproj",
    )(y, u, fo, x, pos, mods3, d_skip.reshape(1, sw), w_glu.astype(BF16), b_glu.reshape(1, sw),
      w_out[:sw].astype(BF16), w_out[sw:].astype(BF16), norm2_g.reshape(1, d))


def _wqs_kernel(sk_ref, wq_ref, o_ref):
    nt = (((1,), (1,)), ((), ()))
    o_ref[...] = lax.dot_general(sk_ref[...], wq_ref[...], nt,
                                 preferred_element_type=F32, precision=HIGHEST)


def _fold_query_keys(w_q, sub_keys):
    d = w_q.shape[0]
    nh, _, nk, half = sub_keys.shape
    units = 2 * nh
    out = pl.pallas_call(
        _wqs_kernel,
        grid=(units,),
        in_specs=[pl.BlockSpec((None, None, nk, half), lambda u: (u % nh, u // nh, 0, 0)),
                  pl.BlockSpec((d, half), lambda u: (0, (u % nh) * 2 + u // nh))],
        out_specs=pl.BlockSpec((None, nk, d), lambda u: (u, 0, 0)),
        out_shape=jax.ShapeDtypeStruct((units, nk, d), F32),
        compiler_params=_cparams("arbitrary"),
        name="fold_query_keys",
    )(sub_keys, w_q)
    return out.transpose(1, 0, 2).reshape(nk * units, d).astype(BF16)


def _oddeven_merge_sort_pairs(n):
    pairs = []
    p = 1
    while p < n:
        k = p
        while k >= 1:
            for j in range(k % p, n - k, 2 * k):
                for i in range(min(k, n - j - k)):
                    if (i + j) // (2 * p) == (i + j + k) // (2 * p):
                        pairs.append((i + j, i + j + k))
            k //= 2
        p *= 2
    return pairs


_SORT16 = _oddeven_merge_sort_pairs(PEER_TOPK)


def _sort_desc(v):
    v = list(v)
    for i, j in _SORT16:
        hi = jnp.maximum(v[i], v[j])
        lo = jnp.minimum(v[i], v[j])
        v[i], v[j] = hi, lo
    return v


def _merge_top(a, b):
    n = len(a)
    c = [jnp.maximum(a[i], b[n - 1 - i]) for i in range(n)]
    k = n // 2
    while k >= 1:
        for i in range(n):
            if (i & k) == 0:
                hi = jnp.maximum(c[i], c[i + k])
                lo = jnp.minimum(c[i], c[i + k])
                c[i], c[i + k] = hi, lo
        k //= 2
    return c


def _top_sorted(vals):
    k = PEER_TOPK
    runs = [_sort_desc(vals[i:i + k]) for i in range(0, len(vals), k)]
    while len(runs) > 1:
        runs = [_merge_top(runs[i], runs[i + 1]) for i in range(0, len(runs), 2)]
    return runs[0]


def _cand_cells():
    k = PEER_TOPK
    return [(r0, r1) for r0 in range(k) for r1 in range(k) if (r0 + 1) * (r1 + 1) <= k]


_CELLS = _cand_cells()


def _select_block(s0, s1):
    k = PEER_TOPK
    shape = s0[0].shape
    one = jnp.ones(shape, F32)
    zero = jnp.zeros(shape, F32)
    sv0 = _top_sorted(s0)
    sv1 = _top_sorted(s1)
    cand = {c: sv0[c[0]] + sv1[c[1]] for c in _CELLS}
    cnt = {}
    for c in _CELLS:
        cnt[c] = float(sum(1 for o in _CELLS if o != c and o[0] <= c[0] and o[1] <= c[1]))
    acc = {c: None for c in _CELLS}
    for a_i, ca in enumerate(_CELLS):
        for cb in _CELLS[a_i + 1:]:
            if ca[0] <= cb[0] and ca[1] <= cb[1]:
                continue
            first = cand[ca] >= cand[cb]
            ib = jnp.where(first, one, zero)
            acc[cb] = ib if acc[cb] is None else acc[cb] + ib
            ia = one - ib
            acc[ca] = ia if acc[ca] is None else acc[ca] + ia
    e0 = [jnp.exp(sv0[r] - sv0[0]) for r in range(k)]
    e1 = [jnp.exp(sv1[r] - sv1[0]) for r in range(k)]
    nsel = [zero] * k
    zsum = zero
    for c in _CELLS:
        tot = acc[c] + cnt[c] if acc[c] is not None else jnp.full(shape, cnt[c], F32)
        sel = jnp.where(tot < float(k), one, zero)
        nsel[c[0]] = nsel[c[0]] + sel
        zsum = zsum + sel * (e0[c[0]] * e1[c[1]])
    inv_z = 1.0 / zsum
    n_l, ex0_l, r_l, ex1_l = [], [], [], []
    for e in range(len(s0)):
        n_e = zero
        r_e = jnp.full(shape, float(k), F32)
        for r in reversed(range(k)):
            n_e = jnp.where(s0[e] >= sv0[r], nsel[r], n_e)
            r_e = jnp.where(s1[e] >= sv1[r], float(r), r_e)
        n_l.append(n_e)
        r_l.append(r_e)
        ex0_l.append(jnp.exp(s0[e] - sv0[0]) * inv_z)
        ex1_l.append(jnp.exp(s1[e] - sv1[0]))
    return n_l, ex0_l, r_l, ex1_l


def _select_kernel(wqs_ref, h2_ref, n0_ref, ex0_ref, r1_ref, ex1_ref, s_s, *, tm):
    nh = PEER_HEADS
    nk = PEER_KEYS
    units = 2 * nh
    nt = (((1,), (1,)), ((), ()))
    s_s[...] = lax.dot_general(wqs_ref[...], h2_ref[...], nt, preferred_element_type=F32)

    def block(j, carry):
        c0 = pl.multiple_of(j * LANES, LANES)
        cols = pl.ds(c0, LANES)
        s0 = [s_s[e * units:e * units + nh, cols] for e in range(nk)]
        s1 = [s_s[e * units + nh:(e + 1) * units, cols] for e in range(nk)]
        n_l, ex0_l, r_l, ex1_l = _select_block(s0, s1)
        for e in range(nk):
            n0_ref[e, :, cols] = n_l[e]
            ex0_ref[e, :, cols] = ex0_l[e]
            for h in range(nh):
                r1_ref[h, pl.ds(e, 1), cols] = r_l[e][h:h + 1, :]
                ex1_ref[h, pl.ds(e, 1), cols] = ex1_l[e][h:h + 1, :]
        return carry

    lax.fori_loop(0, tm // LANES, block, 0)


def _peer_select(wqs, h2, tm):
    tokens, d = h2.shape
    nh = PEER_HEADS
    nk = PEER_KEYS
    rows = wqs.shape[0]
    o_spec = pl.BlockSpec((nk, nh, tm), lambda i: (0, 0, i))
    o_shape = jax.ShapeDtypeStruct((nk, nh, tokens), F32)
    t_spec = pl.BlockSpec((nh, nk, tm), lambda i: (0, 0, i))
    t_shape = jax.ShapeDtypeStruct((nh, nk, tokens), F32)
    return pl.pallas_call(
        functools.partial(_select_kernel, tm=tm),
        grid=(tokens // tm,),
        in_specs=[pl.BlockSpec((rows, d), lambda i: (0, 0)),
                  pl.BlockSpec((tm, d), lambda i: (i, 0))],
        out_specs=[o_spec, o_spec, t_spec, t_spec],
        out_shape=[o_shape, o_shape, t_shape, t_shape],
        scratch_shapes=[pltpu.VMEM((rows, tm), F32)],
        compiler_params=_cparams("parallel"),
        name="peer_select",
    )(wqs, h2)


def _experts_kernel(u_ref, vt_ref, h2_ref, n0_ref, ex0_ref, r1_ref, ex1_ref, x1_ref,
                    mods_ref, fg_ref, o_ref, acc_s, r1_s, ex1_s, w_s, *, d, te, tm, tq):
    nh = PEER_HEADS
    nk = PEER_KEYS
    j = pl.program_id(1)
    nt = (((1,), (1,)), ((), ()))

    @pl.when(j == 0)
    def _():
        acc_s[...] = jnp.zeros_like(acc_s)
        r1_s[...] = r1_ref[...].astype(BF16)
        ex1_s[...] = ex1_ref[...].astype(BF16)

    u = u_ref[...]
    vt = vt_ref[...]
    nq = tm // tq

    def scores(q):
        return lax.dot_general(u, h2_ref[q * tq:(q + 1) * tq, :], nt, preferred_element_type=F32)

    def weights(q):
        cs = slice(q * tq, (q + 1) * tq)
        for el in range(te // nk):
            n0 = n0_ref[el, :, cs].astype(BF16)
            ex0 = ex0_ref[el, :, cs].astype(BF16)
            wgt = jnp.zeros((nk, tq), BF16)
            for h in range(nh):
                ex1 = ex1_s[h, :, cs]
                keep = r1_s[h, :, cs] < n0[h:h + 1, :]
                wgt = wgt + jnp.where(keep, ex1, jnp.zeros_like(ex1)) * ex0[h:h + 1, :]
            w_s[q % 2, el * nk:(el + 1) * nk, :] = wgt

    weights(0)
    a_next = scores(0)
    for q in range(nq):
        cs = slice(q * tq, (q + 1) * tq)
        a = a_next
        if q + 1 < nq:
            a_next = scores(q + 1)
        parts = [jax.nn.gelu(a[el * nk:(el + 1) * nk, :].astype(BF16)) * w_s[q % 2, el * nk:(el + 1) * nk, :]
                 for el in range(te // nk)]
        if q + 1 < nq:
            weights(q + 1)
        wa = jnp.concatenate(parts, axis=0)
        acc_s[:, cs] += jnp.dot(vt, wa, preferred_element_type=F32)

    @pl.when(j == pl.num_programs(1) - 1)
    def _():
        peer = acc_s[...].T
        m = mods_ref[...]
        g2 = m[:, 5 * d:6 * d]
        xo = x1_ref[...] + g2 * peer
        inv = lax.rsqrt(jnp.mean(xo * xo, axis=-1, keepdims=True) + RMS_EPS)
        o_ref[...] = (xo * inv) * fg_ref[...]


def _peer_experts(u_bf, vt_bf, h2, n0, ex0, r1, ex1, x1, mods3, final_g, tm, te, tq):
    bsz, length, d = x1.shape
    tokens = bsz * length
    ne = u_bf.shape[0]
    nh = PEER_HEADS
    nk = PEER_KEYS
    tpb = length // tm
    once = pl.Buffered(1)
    sel0 = pl.BlockSpec((te // nk, nh, tm), lambda i, j: (j, 0, i))
    sel1 = pl.BlockSpec((nh, nk, tm), lambda i, j: (0, 0, i), pipeline_mode=once)
    return pl.pallas_call(
        functools.partial(_experts_kernel, d=d, te=te, tm=tm, tq=tq),
        grid=(tokens // tm, ne // te),
        in_specs=[pl.BlockSpec((te, d), lambda i, j: (j, 0)),
                  pl.BlockSpec((d, te), lambda i, j: (0, j)),
                  pl.BlockSpec((tm, d), lambda i, j: (i, 0), pipeline_mode=once),
                  sel0, sel0, sel1, sel1,
                  pl.BlockSpec((None, tm, d), lambda i, j: (i // tpb, i % tpb, 0), pipeline_mode=once),
                  pl.BlockSpec((None, 1, mods3.shape[2]), lambda i, j: (i // tpb, 0, 0)),
                  pl.BlockSpec((1, d), lambda i, j: (0, 0))],
        out_specs=pl.BlockSpec((None, tm, d), lambda i, j: (i // tpb, i % tpb, 0)),
        out_shape=jax.ShapeDtypeStruct((bsz, length, d), F32),
        scratch_shapes=[pltpu.VMEM((d, tm), F32),
                        pltpu.VMEM((nh, nk, tm), BF16), pltpu.VMEM((nh, nk, tm), BF16),
                        pltpu.VMEM((2, te, tq), BF16)],
        compiler_params=_cparams("parallel", "arbitrary"),
        name="peer_experts",
    )(u_bf, vt_bf, h2, n0, ex0, r1, ex1, x1, mods3, final_g.reshape(1, d))


def kernel(x, c, ctx, c_ctx, w_mod, b_mod, norm1_g, norm2_g, w_in, w_out, s5_lam_re, s5_lam_im,
           s5_log_dt, s5_b_re, s5_b_im, s5_c_re, s5_c_im, s5_d, w_glu, b_glu, w_fourier,
           peer_w_q, peer_sub_keys, peer_u, peer_v, final_g):
    bsz, length, d = x.shape
    clen = ctx.shape[1]
    sw = s5_d.shape[-1]
    fw = w_in.shape[-1] - sw
    assert w_mod.shape[0] == 1, "single layer: the context stream only feeds the S5 initial states"
    assert length % CHUNK == 0 and clen % CHUNK == 0 and bsz % 8 == 0

    tl = 1024 // bsz
    tm = min(512, length)
    te = 1024
    assert tl % CHUNK == 0 and length % tl == 0 and clen % tl == 0

    pos = jnp.asarray(_pos_embed(length // GRID_W, d))

    mrows = -(-(bsz + 1) // 8) * 8
    c_rows = jnp.concatenate([c, c_ctx[None, :], jnp.zeros((mrows - bsz - 1, d), F32)], axis=0)
    mods = _modulation(c_rows, w_mod[0], b_mod[0])
    mods3 = mods.reshape(mrows, 1, N_MOD * d)

    wfc, wfs = _fourier_fold(w_in[0][:, sw:], w_fourier[0])
    w_all = jnp.concatenate([w_in[0][:, :sw], wfc, wfs], axis=1).astype(BF16)

    u3, gc, gs = _inproj(x, pos, mods3, None, norm1_g[0], w_all, (sw, fw, fw), (BF16, BF16), tl)
    (uc3,) = _inproj(ctx, None, mods3, bsz, norm1_g[0], w_all[:, :sw], (sw,), (), tl)

    wm, wp, qt, a16 = _s5_prep(s5_lam_re[0], s5_lam_im[0], s5_log_dt[0], s5_b_re[0], s5_b_im[0],
                               s5_c_re[0], s5_c_im[0])
    y3 = _s5_scan(u3, uc3, wm, wp, qt, a16, bsz)

    fo = _seq_dft(gc, gs, 1.0 / math.sqrt(length * (fw // FOURIER_GROUPS)), min(512, length))

    x1, h2 = _outproj(y3, u3, fo, x, pos, mods3, s5_d[0], w_glu[0], b_glu[0], w_out[0], norm2_g[0], tl)

    wqs = _fold_query_keys(peer_w_q[0], peer_sub_keys[0])
    h2f = h2.reshape(bsz * length, d)
    n0, ex0, r1, ex1 = _peer_select(wqs, h2f, tm)
    u_bf = peer_u[0].astype(BF16)
    vt_bf = peer_v[0].astype(BF16).T
    tme = min(1024, length)
    return _peer_experts(u_bf, vt_bf, h2f, n0, ex0, r1, ex1, x1, mods3, final_g, tme, te, min(256, tme))
```
